```python
import math
import jax
import jax.numpy as jnp
from jax import lax
import numpy as np

D_MODEL = 1024
BATCH = 16
SEQ = 2048
DEPTH = 2

GDN_HEADS = D_MODEL // 256
GDN_DK = 128
GDN_DV = 128
GDN_QK_W = GDN_HEADS * GDN_DK
GDN_W = GDN_HEADS * GDN_DV
CONV_K = 4
CHUNK = 64
DIFF_HEADS = D_MODEL // 256
DIFF_D = 64
DIFF_W = DIFF_HEADS * 2 * DIFF_D
Q_BLOCK = 128
MIX_W = GDN_W + DIFF_W
D_FF = ((8 * D_MODEL // 3 + 127) // 128) * 128
ROPE_THETA = 10000.0
EPS = 1e-6
IN_SPLITS = (2 * GDN_QK_W + GDN_W,
             GDN_W,
             GDN_HEADS,
             GDN_HEADS,
             DIFF_W,
             DIFF_W,
             DIFF_W)
N_IN = sum(IN_SPLITS)

kernel_name = 'hybrid_gdn_diffattn_macaron'


def lambda_init(layer):
    return 0.8 - 0.6 * math.exp(-0.3 * layer)


def rms_norm(x, w):
    xf = x.astype(jnp.float32)
    y = xf * lax.rsqrt(jnp.mean(xf * xf, axis=-1, keepdims=True) + EPS)
    return (y * w.astype(jnp.float32)).astype(x.dtype)


def l2norm(x):
    return x * lax.rsqrt(jnp.sum(x * x, axis=-1, keepdims=True) + EPS)


def swiglu(h, w_gate, w_up, w_down):
    return (jax.nn.silu(h @ w_gate) * (h @ w_up)) @ w_down


def rope_tables(seq):
    inv = 1.0 / (ROPE_THETA ** (jnp.arange(0, DIFF_D, 2, dtype=jnp.float32) / DIFF_D))
    ang = jnp.arange(seq, dtype=jnp.float32)[:, None] * inv[None, :]
    return jnp.cos(ang), jnp.sin(ang)


def apply_rope(x, cos, sin):
    xf = x.astype(jnp.float32)
    x1, x2 = jnp.split(xf, 2, axis=-1)
    c = cos[None, :, None, None, :]
    s = sin[None, :, None, None, :]
    return jnp.concatenate([x1 * c - x2 * s, x2 * c + x1 * s], axis=-1).astype(x.dtype)


def causal_conv_silu(x, w):
    c = x.shape[-1]
    y = lax.conv_general_dilated(x, w[:, None, :].astype(x.dtype), window_strides=(1,),
                                 padding=[(CONV_K - 1, 0)],
                                 dimension_numbers=('NWC', 'WIO', 'NWC'),
                                 feature_group_count=c)
    return jax.nn.silu(y)


def chunk_gated_delta_rule(q, k, v, g, beta):
    b, h, s, dk = q.shape
    dv = v.shape[-1]
    n = s // CHUNK
    q = q.reshape(b, h, n, CHUNK, dk)
    k = k.reshape(b, h, n, CHUNK, dk)
    v = v.reshape(b, h, n, CHUNK, dv)
    gc = jnp.cumsum(g.reshape(b, h, n, CHUNK), axis=-1)
    beta = beta.reshape(b, h, n, CHUNK)
    idx = jnp.arange(CHUNK)
    lower = idx[:, None] >= idx[None, :]
    strict = idx[:, None] > idx[None, :]
    decay_mat = jnp.exp(jnp.where(lower, gc[..., :, None] - gc[..., None, :], -jnp.inf))
    k_beta = k * beta[..., None]
    m = jnp.where(strict, jnp.einsum('bhncd,bhnmd->bhncm', k_beta, k) * decay_mat, 0.0)
    eye = jnp.eye(CHUNK, dtype=q.dtype)
    rhs = jnp.concatenate([v * beta[..., None], k_beta * jnp.exp(gc)[..., None]], axis=-1)
    sol = lax.linalg.triangular_solve(eye + m, rhs, left_side=True, lower=True, unit_diagonal=True)
    u, w = sol[..., :dv], sol[..., dv:]
    qk = jnp.einsum('bhncd,bhnmd->bhncm', q, k) * decay_mat
    q_dec = q * jnp.exp(gc)[..., None]
    g_last = gc[..., -1]
    k_dec = k * jnp.exp(g_last[..., None] - gc)[..., None]

    def step(state, xs):
        q_i, k_i, u_i, w_i, qk_i, gl_i = xs
        v_new = u_i - jnp.einsum('bhcd,bhde->bhce', w_i, state)
        o_i = jnp.einsum('bhcd,bhde->bhce', q_i, state) + jnp.einsum('bhcm,bhme->bhce', qk_i, v_new)
        state = state * jnp.exp(gl_i)[..., None, None] + jnp.einsum('bhcd,bhce->bhde', k_i, v_new)
        return state, o_i

    xs = tuple(jnp.moveaxis(t, 2, 0) for t in (q_dec, k_dec, u, w, qk, g_last))
    state0 = jnp.zeros((b, h, dk, dv), q.dtype)
    _, o = lax.scan(step, state0, xs)
    return jnp.moveaxis(o, 0, 2).reshape(b, h, s, dv)


def gdn_group(qkv, gate, alpha, beta_logit, conv_w, a_log, dt_bias, norm_w):
    f32 = jnp.float32
    b, s, _ = qkv.shape
    out_dtype = qkv.dtype
    qkv = causal_conv_silu(qkv, conv_w).astype(f32)
    q, k, v = jnp.split(qkv, [GDN_QK_W, 2 * GDN_QK_W], axis=-1)
    q = l2norm(q.reshape(b, s, GDN_HEADS, GDN_DK).transpose(0, 2, 1, 3)) * GDN_DK ** -0.5
    k = l2norm(k.reshape(b, s, GDN_HEADS, GDN_DK).transpose(0, 2, 1, 3))
    v = v.reshape(b, s, GDN_HEADS, GDN_DV).transpose(0, 2, 1, 3)
    g = (-jnp.exp(a_log.astype(f32)) * jax.nn.softplus(alpha.astype(f32) + dt_bias.astype(f32))).transpose(0, 2, 1)
    beta = jax.nn.sigmoid(beta_logit.astype(f32)).transpose(0, 2, 1)
    o = chunk_gated_delta_rule(q, k, v, g, beta).transpose(0, 2, 1, 3)
    o = rms_norm(o, norm_w) * jax.nn.silu(gate.astype(f32).reshape(b, s, GDN_HEADS, GDN_DV))
    return o.reshape(b, s, GDN_W).astype(out_dtype)


def diff_group(q, k, v, lq1, lk1, lq2, lk2, norm_w, cos, sin, lam_init):
    f32 = jnp.float32
    b, s, _ = q.shape
    q = apply_rope(q.reshape(b, s, DIFF_HEADS, 2, DIFF_D), cos, sin).transpose(0, 2, 3, 1, 4)
    k = apply_rope(k.reshape(b, s, DIFF_HEADS, 2, DIFF_D), cos, sin).transpose(0, 2, 3, 1, 4)
    v = v.reshape(b, s, DIFF_HEADS, 2 * DIFF_D).transpose(0, 2, 1, 3)
    lam = (jnp.exp(jnp.sum(lq1.astype(f32) * lk1.astype(f32)))
           - jnp.exp(jnp.sum(lq2.astype(f32) * lk2.astype(f32))) + lam_init)
    n_blk = s // Q_BLOCK
    q_blocks = q.reshape(b, DIFF_HEADS, 2, n_blk, Q_BLOCK, DIFF_D).transpose(3, 0, 1, 2, 4, 5)
    key_pos = jnp.arange(s)
    scale = DIFF_D ** -0.5

    def attend(args):
        q_b, start = args
        sc = jnp.einsum('bhcqd,bhckd->bhcqk', q_b, k, preferred_element_type=f32) * scale
        q_pos = start + jnp.arange(Q_BLOCK)
        sc = jnp.where(key_pos[None, :] <= q_pos[:, None], sc, -jnp.inf)
        p = jax.nn.softmax(sc, axis=-1)
        a = p[:, :, 0] - lam * p[:, :, 1]
        return jnp.einsum('bhqk,bhke->bhqe', a.astype(v.dtype), v)

    o = lax.map(attend, (q_blocks, jnp.arange(n_blk) * Q_BLOCK))
    o = o.transpose(1, 0, 3, 2, 4).reshape(b, s, DIFF_HEADS, 2 * DIFF_D)
    o = rms_norm(o, norm_w) * (1.0 - lam_init)
    return o.reshape(b, s, DIFF_W)


def setup_inputs(seed: int = 0) -> dict:
    key = jax.random.key(seed)
    ks = iter(jax.random.split(key, 32))
    L = DEPTH
    conv_c = 2 * GDN_QK_W + GDN_W

    def nrm(k, shape, s):
        return jax.random.normal(k, shape, jnp.float32) * s

    def gain(k, shape):
        return 1.0 + 0.02 * jax.random.normal(k, shape, jnp.float32)

    dt = jnp.exp(jax.random.uniform(next(ks), (L, GDN_HEADS), jnp.float32,
                                    minval=math.log(1e-3), maxval=math.log(1e-1)))
    a_init = jax.random.uniform(next(ks), (L, GDN_HEADS), jnp.float32, minval=1.0, maxval=16.0)
    return {
        'x': nrm(next(ks), (BATCH, SEQ, D_MODEL), 1.0),
        'ffn1_norm': gain(next(ks), (L, D_MODEL)),
        'ffn1_w_gate': nrm(next(ks), (L, D_MODEL, D_FF), D_MODEL ** -0.5),
        'ffn1_w_up': nrm(next(ks), (L, D_MODEL, D_FF), D_MODEL ** -0.5),
        'ffn1_w_down': nrm(next(ks), (L, D_FF, D_MODEL), D_FF ** -0.5),
        'mix_norm': gain(next(ks), (L, D_MODEL)),
        'w_in': nrm(next(ks), (L, D_MODEL, N_IN), D_MODEL ** -0.5),
        'conv_w': nrm(next(ks), (L, CONV_K, conv_c), CONV_K ** -0.5),
        'a_log': jnp.log(a_init),
        'dt_bias': dt + jnp.log(-jnp.expm1(-dt)),
        'gdn_norm': gain(next(ks), (L, GDN_DV)),
        'lambda_q1': nrm(next(ks), (L, DIFF_D), 0.1),
        'lambda_k1': nrm(next(ks), (L, DIFF_D), 0.1),
        'lambda_q2': nrm(next(ks), (L, DIFF_D), 0.1),
        'lambda_k2': nrm(next(ks), (L, DIFF_D), 0.1),
        'diff_norm': gain(next(ks), (L, 2 * DIFF_D)),
        'w_out': nrm(next(ks), (L, MIX_W, D_MODEL), MIX_W ** -0.5),
        'ffn2_norm': gain(next(ks), (L, D_MODEL)),
        'ffn2_w_gate': nrm(next(ks), (L, D_MODEL, D_FF), D_MODEL ** -0.5),
        'ffn2_w_up': nrm(next(ks), (L, D_MODEL, D_FF), D_MODEL ** -0.5),
        'ffn2_w_down': nrm(next(ks), (L, D_FF, D_MODEL), D_FF ** -0.5),
        'final_norm': gain(next(ks), (D_MODEL,)),
    }


def reference(x, ffn1_norm, ffn1_w_gate, ffn1_w_up, ffn1_w_down, mix_norm, w_in, conv_w,
              a_log, dt_bias, gdn_norm, lambda_q1, lambda_k1, lambda_q2, lambda_k2, diff_norm,
              w_out, ffn2_norm, ffn2_w_gate, ffn2_w_up, ffn2_w_down, final_norm):
    _, s, _ = x.shape
    cos, sin = rope_tables(s)
    split_at = np.cumsum(IN_SPLITS)[:-1].tolist()
    for l in range(DEPTH):
        x = x + 0.5 * swiglu(rms_norm(x, ffn1_norm[l]), ffn1_w_gate[l], ffn1_w_up[l], ffn1_w_down[l])
        h = rms_norm(x, mix_norm[l])
        proj = h @ w_in[l]
        qkv_a, gate_a, alpha_a, beta_a, q_d, k_d, v_d = jnp.split(proj, split_at, axis=-1)
        o_a = gdn_group(qkv_a, gate_a, alpha_a, beta_a, conv_w[l], a_log[l], dt_bias[l], gdn_norm[l])
        o_d = diff_group(q_d, k_d, v_d, lambda_q1[l], lambda_k1[l], lambda_q2[l], lambda_k2[l],
                         diff_norm[l], cos, sin, lambda_init(l))
        x = x + jnp.concatenate([o_a, o_d], axis=-1) @ w_out[l]
        x = x + 0.5 * swiglu(rms_norm(x, ffn2_norm[l]), ffn2_w_gate[l], ffn2_w_up[l], ffn2_w_down[l])
    return rms_norm(x, final_norm)
```

```python
import functools
import math

import jax
import jax.numpy as jnp
from jax import lax
from jax.experimental import pallas as pl
from jax.experimental.pallas import tpu as pltpu

F32 = jnp.float32
BF16 = jnp.bfloat16

EPS = 1e-6
ROPE_THETA = 10000.0
HEAD_W = 128
N_HEADS = 4
GROUP_W = N_HEADS * HEAD_W
DIFF_D = 64
CONV_K = 4
HALO = 8
GDN_CHUNK = 128
VMEM_LIMIT = 56 * 1024 * 1024


def _dot(a, b):
    return jnp.dot(a, b, preferred_element_type=F32)


def _dot_nt(a, b):
    return lax.dot_general(a, b, (((1,), (1,)), ((), ())), preferred_element_type=F32)


def _rms(x, w):
    return x * lax.rsqrt(jnp.mean(x * x, axis=-1, keepdims=True) + EPS) * w


def _sigmoid(x):
    return 1.0 / (1.0 + jnp.exp(-x))


def _silu(x):
    return x * _sigmoid(x)


def _resident(shape):
    return pl.BlockSpec(shape, lambda *_: (0,) * len(shape), pipeline_mode=pl.Buffered(1))


def _ffn_body(*refs, tf, pre_proj, post_norm):
    it = iter(refs)
    x_ref = next(it)
    if pre_proj:
        oa_ref, od_ref, woa_ref, wod_ref = next(it), next(it), next(it), next(it)
    nw_ref, wg_ref, wu_ref, wd_ref = next(it), next(it), next(it), next(it)
    if post_norm:
        fw_ref = next(it)
    o_ref, a_scr = next(it), next(it)

    x = x_ref[...]
    if pre_proj:
        x = x + _dot(oa_ref[...], woa_ref[...]) + _dot(od_ref[...], wod_ref[...])
    h = _rms(x, nw_ref[...]).astype(BF16)
    d_ff = wg_ref.shape[1]
    for c in range(d_ff // tf):
        cols = slice(c * tf, (c + 1) * tf)
        g = _dot(h, wg_ref[:, cols])
        u = _dot(h, wu_ref[:, cols])
        a_scr[:, cols] = (_silu(g) * u).astype(BF16)
    out = x + 0.5 * _dot(a_scr[...], wd_ref[...])
    if post_norm:
        out = _rms(out, fw_ref[...])
    o_ref[...] = out


def _ffn(x, nw, wg, wu, wd, *, proj=None, final_w=None, tm, tf=256):
    t, d = x.shape
    d_ff = wg.shape[1]
    row = lambda w: pl.BlockSpec((tm, w), lambda i: (i, 0))
    args, specs = [x], [row(d)]
    if proj is not None:
        oa, od, woa, wod = proj
        args += [oa, od, woa, wod]
        specs += [row(oa.shape[1]), row(od.shape[1]), _resident(woa.shape), _resident(wod.shape)]
    args += [nw, wg, wu, wd]
    specs += [_resident(nw.shape), _resident(wg.shape), _resident(wu.shape), _resident(wd.shape)]
    if final_w is not None:
        args.append(final_w)
        specs.append(_resident(final_w.shape))
    body = functools.partial(_ffn_body, tf=tf, pre_proj=proj is not None, post_norm=final_w is not None)
    return pl.pallas_call(
        body,
        grid=(t // tm,),
        in_specs=specs,
        out_specs=row(d),
        out_shape=jax.ShapeDtypeStruct((t, d), F32),
        scratch_shapes=[pltpu.VMEM((tm, d_ff), BF16)],
        compiler_params=pltpu.CompilerParams(dimension_semantics=("parallel",), vmem_limit_bytes=VMEM_LIMIT),
        name="ffn_proj" if proj is not None else "ffn",
    )(*args)


def _in_proj_body(x_ref, nw_ref, wqkv_ref, wgate_ref, wab_ref, wqd_ref, wkd_ref, wvd_ref,
                  cw_ref, alog_ref, dtb_ref, cos_ref, sin_ref,
                  qa_ref, ka_ref, va_ref, gate_ref, gb_ref, qd_ref, kd_ref, vd_ref,
                  pbuf, *, blocks_per_seq):
    tm = x_ref.shape[0]
    h = _rms(x_ref[...], nw_ref[...]).astype(BF16)

    first = pl.program_id(0) % blocks_per_seq == 0
    for sec, out_ref in enumerate((qa_ref, ka_ref, va_ref)):
        cols = slice(sec * GROUP_W, (sec + 1) * GROUP_W)

        @pl.when(first)
        def _():
            pbuf[0:HALO, cols] = jnp.zeros((HALO, GROUP_W), F32)

        p = _dot(h, wqkv_ref[:, cols])
        pbuf[HALO:HALO + tm, cols] = p
        y = p * cw_ref[CONV_K - 1:CONV_K, cols]
        for j in range(CONV_K - 1):
            shift = CONV_K - 1 - j
            y = y + pbuf[HALO - shift:HALO - shift + tm, cols] * cw_ref[j:j + 1, cols]
        pbuf[0:HALO, cols] = pbuf[tm:tm + HALO, cols]
        y = _silu(y)
        if sec < 2:
            scale = HEAD_W ** -0.5 if sec == 0 else 1.0
            for hd in range(N_HEADS):
                hc = slice(hd * HEAD_W, (hd + 1) * HEAD_W)
                yh = y[:, hc]
                yh = yh * (lax.rsqrt(jnp.sum(yh * yh, axis=-1, keepdims=True) + EPS) * scale)
                out_ref[:, hc] = yh.astype(BF16)
        else:
            out_ref[...] = y.astype(BF16)

    gate_ref[...] = _dot(h, wgate_ref[...]).astype(BF16)

    ab = _dot(h, wab_ref[...])
    z = ab + dtb_ref[...]
    softplus = jnp.maximum(z, 0.0) + jnp.log1p(jnp.exp(-jnp.abs(z)))
    g = -jnp.exp(alog_ref[...]) * softplus
    lane = lax.broadcasted_iota(jnp.int32, ab.shape, 1)
    gb_ref[...] = jnp.where(lane < N_HEADS, g, _sigmoid(ab))

    cos = cos_ref[...]
    sin = sin_ref[...]
    lane = lax.broadcasted_iota(jnp.int32, (tm, HEAD_W), 1)
    low_half = (lane % DIFF_D) < (DIFF_D // 2)
    for w_ref, out_ref, scale in ((wqd_ref, qd_ref, DIFF_D ** -0.5), (wkd_ref, kd_ref, 1.0)):
        p = _dot(h, w_ref[...])
        for hd in range(N_HEADS):
            hc = slice(hd * HEAD_W, (hd + 1) * HEAD_W)
            ph = p[:, hc]
            partner = jnp.where(low_half,
                                pltpu.roll(ph, HEAD_W - DIFF_D // 2, 1),
                                pltpu.roll(ph, DIFF_D // 2, 1))
            out_ref[:, hc] = ((ph * cos + partner * sin) * scale).astype(BF16)
    vd_ref[...] = _dot(h, wvd_ref[...]).astype(BF16)


def _in_proj(x, nw, wqkv, wgate, wab, wqd, wkd, wvd, cw, alog, dtb, cos, sin, *, seq, tm):
    t, d = x.shape
    blocks_per_seq = seq // tm
    row = lambda w: pl.BlockSpec((tm, w), lambda i: (i, 0))
    tab = pl.BlockSpec((tm, HEAD_W), lambda i: (i % blocks_per_seq, 0))
    res = [nw, wqkv, wgate, wab, wqd, wkd, wvd, cw, alog, dtb]
    bf = lambda: jax.ShapeDtypeStruct((t, GROUP_W), BF16)
    return pl.pallas_call(
        functools.partial(_in_proj_body, blocks_per_seq=blocks_per_seq),
        grid=(t // tm,),
        in_specs=[row(d)] + [_resident(a.shape) for a in res] + [tab, tab],
        out_specs=[row(GROUP_W)] * 4 + [row(HEAD_W)] + [row(GROUP_W)] * 3,
        out_shape=[bf(), bf(), bf(), bf(), jax.ShapeDtypeStruct((t, HEAD_W), F32), bf(), bf(), bf()],
        scratch_shapes=[pltpu.VMEM((tm + HALO, 3 * GROUP_W), F32)],
        compiler_params=pltpu.CompilerParams(dimension_semantics=("arbitrary",), vmem_limit_bytes=VMEM_LIMIT),
        name="in_proj",
    )(x, *res, cos, sin)


def _split3(x):
    hi = x.astype(BF16)
    r = x - hi.astype(F32)
    mid = r.astype(BF16)
    lo = (r - mid.astype(F32)).astype(BF16)
    return hi, mid, lo


def _gdn_body(q_ref, k_ref, v_ref, gate_ref, gb_ref, nw_ref, o_ref, state):
    c = GDN_CHUNK
    n_chunks = q_ref.shape[0] // c
    row = lax.broadcasted_iota(jnp.int32, (c, c), 0)
    col = lax.broadcasted_iota(jnp.int32, (c, c), 1)
    incl = (row >= col)
    strict = (row > col)
    xor = row ^ col
    pair_level = jnp.full((c, c), -1, jnp.int32)
    for level in range(int(math.log2(c))):
        pair_level = jnp.where((xor >> level) == 1, level, pair_level)
    ones_tri = jnp.where(incl, 1.0, 0.0).astype(BF16)
    nw = nw_ref[...]
    state[...] = jnp.zeros(state.shape, F32)

    def chunk(i, carry):
        rows = pl.ds(pl.multiple_of(i * c, c), c)
        gb = gb_ref[rows, :]
        g_hi, g_mid, g_lo = _split3(gb)
        gc = _dot(ones_tri, g_hi) + _dot(ones_tri, g_mid) + _dot(ones_tri, g_lo)
        gc_t = gc.T
        g_last = gc[c - 1:c, :]
        e_gc = jnp.exp(gc)
        e_rest = jnp.exp(g_last - gc)
        e_last = jnp.exp(g_last)
        for hd in range(N_HEADS):
            hc = slice(hd * HEAD_W, (hd + 1) * HEAD_W)
            q = q_ref[rows, hc]
            k = k_ref[rows, hc]
            kf = k.astype(F32)
            beta = gb[:, N_HEADS + hd:N_HEADS + hd + 1]
            diff = gc[:, hd:hd + 1] - gc_t[hd:hd + 1, :]
            decay = jnp.where(incl, jnp.exp(jnp.minimum(diff, 0.0)), 0.0)
            kb = kf * beta
            prod = _dot_nt(jnp.concatenate([kb.astype(BF16), q], axis=0), k)
            m = jnp.where(strict, prod[:c] * decay, 0.0)
            qk = prod[c:] * decay
            rhs = jnp.concatenate([v_ref[rows, hc].astype(F32) * beta, kb * e_gc[:, hd:hd + 1]], axis=1)
            t_inv = jnp.where(row == col, 1.0, 0.0) - jnp.where(pair_level == 0, m, 0.0)
            for level in range(1, int(math.log2(c))):
                c_t = _dot(jnp.where(pair_level == level, m, 0.0).astype(BF16), t_inv.astype(BF16))
                t_inv = t_inv - _dot(t_inv.astype(BF16), c_t.astype(BF16))
            sol = _dot(t_inv.astype(BF16), rhs.astype(BF16))
            u = sol[:, :HEAD_W]
            w = sol[:, HEAD_W:]
            s = state[hd]
            q_dec = q.astype(F32) * e_gc[:, hd:hd + 1]
            ws_qs = _dot(jnp.concatenate([w.astype(BF16), q_dec.astype(BF16)], axis=0), s.astype(BF16))
            v_new = (u - ws_qs[:c]).astype(BF16)
            o = ws_qs[c:] + _dot(qk.astype(BF16), v_new)
            k_dec_t = (kf * e_rest[:, hd:hd + 1]).T.astype(BF16)
            state[hd] = s * e_last[:, hd:hd + 1] + _dot(k_dec_t, v_new)
            o = _rms(o, nw) * _silu(gate_ref[rows, hc].astype(F32))
            o_ref[rows, hc] = o.astype(BF16)
        return carry

    lax.fori_loop(0, n_chunks, chunk, 0)


def _gdn(q, k, v, gate, gb, nw, *, seq):
    t = q.shape[0]
    seq_blk = lambda w: pl.BlockSpec((seq, w), lambda b: (b, 0))
    return pl.pallas_call(
        _gdn_body,
        grid=(t // seq,),
        in_specs=[seq_blk(GROUP_W)] * 4 + [seq_blk(HEAD_W), _resident(nw.shape)],
        out_specs=seq_blk(GROUP_W),
        out_shape=jax.ShapeDtypeStruct((t, GROUP_W), BF16),
        scratch_shapes=[pltpu.VMEM((N_HEADS, HEAD_W, HEAD_W), F32)],
        compiler_params=pltpu.CompilerParams(dimension_semantics=("parallel",), vmem_limit_bytes=VMEM_LIMIT),
        name="gdn",
    )(q, k, v, gate, gb, nw)


def _diff_body(q_ref, k_ref, v_ref, lq1_ref, lk1_ref, lq2_ref, lk2_ref, nw_ref, o_ref, *, lam_init):
    tq = q_ref.shape[0]
    i = pl.program_id(2)
    q = q_ref[...]
    lane = lax.broadcasted_iota(jnp.int32, q.shape, 1)
    zero = jnp.zeros_like(q)
    q2 = jnp.concatenate([jnp.where(lane < DIFF_D, q, zero), jnp.where(lane >= DIFF_D, q, zero)], axis=0)

    def step(rows, masked, carry):
        m, l, acc = carry
        s = _dot_nt(q2, k_ref[rows, :])
        if masked:
            r = lax.broadcasted_iota(jnp.int32, (tq, tq), 0)
            cidx = lax.broadcasted_iota(jnp.int32, (tq, tq), 1)
            keep = jnp.concatenate([cidx <= r, cidx <= r], axis=0)
            s = jnp.where(keep, s, -jnp.inf)
        m_new = jnp.maximum(m, jnp.max(s, axis=-1, keepdims=True))
        alpha = jnp.exp(m - m_new)
        p = jnp.exp(s - m_new)
        l = alpha * l + jnp.sum(p, axis=-1, keepdims=True)
        acc = alpha * acc + _dot(p.astype(BF16), v_ref[rows, :])
        return m_new, l, acc

    init = (jnp.full((2 * tq, 1), -jnp.inf, F32), jnp.zeros((2 * tq, 1), F32), jnp.zeros((2 * tq, HEAD_W), F32))
    carry = lax.fori_loop(
        0, i, lambda j, cr: step(pl.ds(pl.multiple_of(j * tq, tq), tq), False, cr), init)
    _, l, acc = step(pl.ds(pl.multiple_of(i * tq, tq), tq), True, carry)

    lam = (jnp.exp(jnp.sum(lq1_ref[...] * lk1_ref[...], axis=-1, keepdims=True))
           - jnp.exp(jnp.sum(lq2_ref[...] * lk2_ref[...], axis=-1, keepdims=True)) + lam_init)
    o = acc / l
    o = o[:tq] - lam * o[tq:]
    o_ref[...] = (_rms(o, nw_ref[...]) * (1.0 - lam_init)).astype(BF16)


def _diff_attn(q, k, v, lq1, lk1, lq2, lk2, nw, *, seq, lam_init, tq):
    t = q.shape[0]
    nq = seq // tq
    q_blk = pl.BlockSpec((tq, HEAD_W), lambda b, h, i: (b * nq + i, h))
    kv_blk = pl.BlockSpec((seq, HEAD_W), lambda b, h, i: (b, h))
    small = [lq1, lk1, lq2, lk2, nw]
    return pl.pallas_call(
        functools.partial(_diff_body, lam_init=lam_init),
        grid=(t // seq, N_HEADS, nq),
        in_specs=[q_blk, kv_blk, kv_blk] + [_resident(a.shape) for a in small],
        out_specs=q_blk,
        out_shape=jax.ShapeDtypeStruct((t, GROUP_W), BF16),
        compiler_params=pltpu.CompilerParams(
            dimension_semantics=("parallel", "parallel", "arbitrary"), vmem_limit_bytes=VMEM_LIMIT),
        name="diff_attn",
    )(q, k, v, *small)


def _rope_tables(seq):
    inv = 1.0 / (ROPE_THETA ** (jnp.arange(0, DIFF_D, 2, dtype=F32) / DIFF_D))
    ang = jnp.arange(seq, dtype=F32)[:, None] * inv[None, :]
    cos, sin = jnp.cos(ang), jnp.sin(ang)
    reps = HEAD_W // DIFF_D
    return (jnp.tile(jnp.concatenate([cos, cos], axis=-1), (1, reps)),
            jnp.tile(jnp.concatenate([-sin, sin], axis=-1), (1, reps)))


def _lambda_init(layer):
    return 0.8 - 0.6 * math.exp(-0.3 * layer)


def kernel(x, ffn1_norm, ffn1_w_gate, ffn1_w_up, ffn1_w_down, mix_norm, w_in, conv_w, a_log, dt_bias, gdn_norm, lambda_q1, lambda_k1, lambda_q2, lambda_k2, diff_norm, w_out, ffn2_norm, ffn2_w_gate, ffn2_w_up, ffn2_w_down, final_norm):
    b, seq, d = x.shape
    depth = w_in.shape[0]
    t = b * seq
    tm = min(512, seq)
    tq = min(256, seq)
    assert seq % GDN_CHUNK == 0 and seq % tm == 0 and seq % tq == 0
    n_qkv = 3 * GROUP_W
    bounds = [0, n_qkv, n_qkv + GROUP_W, n_qkv + GROUP_W + N_HEADS, n_qkv + GROUP_W + 2 * N_HEADS]
    bounds += [bounds[-1] + GROUP_W, bounds[-1] + 2 * GROUP_W, bounds[-1] + 3 * GROUP_W]
    assert bounds[-1] == w_in.shape[2]
    cos, sin = _rope_tables(seq)
    row = lambda a: a.reshape(1, -1).astype(F32)
    pad_lanes = lambda a: jnp.pad(a, ((0, 0), (0, HEAD_W - a.shape[1])))
    bf = lambda a: a.astype(BF16)

    xf = x.reshape(t, d).astype(F32)
    for l in range(depth):
        if l == 0:
            xf = _ffn(xf, row(ffn1_norm[l]), bf(ffn1_w_gate[l]), bf(ffn1_w_up[l]), bf(ffn1_w_down[l]), tm=tm)
        wl = w_in[l]
        piece = lambda n: wl[:, bounds[n]:bounds[n + 1]]
        w_ab = pad_lanes(jnp.concatenate([piece(2), piece(3)], axis=1))
        qa, ka, va, gate, gb, qd, kd, vd = _in_proj(
            xf, row(mix_norm[l]), bf(piece(0)), bf(piece(1)), bf(w_ab), bf(piece(4)), bf(piece(5)), bf(piece(6)),
            conv_w[l].astype(F32), pad_lanes(row(a_log[l])), pad_lanes(row(dt_bias[l])), cos, sin, seq=seq, tm=tm)
        o_a = _gdn(qa, ka, va, gate, gb, row(gdn_norm[l]), seq=seq)
        o_d = _diff_attn(qd, kd, vd, row(lambda_q1[l]), row(lambda_k1[l]), row(lambda_q2[l]), row(lambda_k2[l]),
                         row(diff_norm[l]), seq=seq, lam_init=_lambda_init(l), tq=tq)
        wo = w_out[l]
        last = l == depth - 1
        xf = _ffn(xf, row(ffn2_norm[l]), bf(ffn2_w_gate[l]), bf(ffn2_w_up[l]), bf(ffn2_w_down[l]),
                  proj=(o_a, o_d, bf(wo[:GROUP_W]), bf(wo[GROUP_W:])),
                  final_w=row(final_norm) if last else None, tm=tm)
        if not last:
            xf = _ffn(xf, row(ffn1_norm[l + 1]), bf(ffn1_w_gate[l + 1]), bf(ffn1_w_up[l + 1]),
                      bf(ffn1_w_down[l + 1]), tm=tm)
    return xf.reshape(b, seq, d).astype(x.dtype)
```

```python
import functools
import math

import jax
import jax.numpy as jnp
from jax import lax
from jax.experimental import pallas as pl
from jax.experimental.pallas import tpu as pltpu

F32 = jnp.float32
BF16 = jnp.bfloat16

EPS = 1e-6
ROPE_THETA = 10000.0
HEAD_W = 128
N_HEADS = 4
GROUP_W = N_HEADS * HEAD_W
DIFF_D = 64
CONV_K = 4
HALO = 8
GDN_CHUNK = 128
VMEM_LIMIT = 56 * 1024 * 1024


def _dot(a, b):
    return jnp.dot(a, b, preferred_element_type=F32)


def _dot_nt(a, b):
    return lax.dot_general(a, b, (((1,), (1,)), ((), ())), preferred_element_type=F32)


def _rms(x, w):
    return x * lax.rsqrt(jnp.mean(x * x, axis=-1, keepdims=True) + EPS) * w


def _sigmoid(x):
    return 1.0 / (1.0 + jnp.exp(-x))


def _silu(x):
    return x * _sigmoid(x)


def _resident(shape):
    return pl.BlockSpec(shape, lambda *_: (0,) * len(shape), pipeline_mode=pl.Buffered(1))


def _ffn_body(*refs, tf, pre_proj, post_norm):
    it = iter(refs)
    x_ref = next(it)
    if pre_proj:
        oa_ref, od_ref, woa_ref, wod_ref = next(it), next(it), next(it), next(it)
    nw_ref, wg_ref, wu_ref, wd_ref = next(it), next(it), next(it), next(it)
    if post_norm:
        fw_ref = next(it)
    o_ref, a_scr = next(it), next(it)

    x = x_ref[...]
    if pre_proj:
        x = x + _dot(oa_ref[...], woa_ref[...]) + _dot(od_ref[...], wod_ref[...])
    h = _rms(x, nw_ref[...]).astype(BF16)
    d_ff = wg_ref.shape[1]
    for c in range(d_ff // tf):
        cols = slice(c * tf, (c + 1) * tf)
        g = _dot(h, wg_ref[:, cols])
        u = _dot(h, wu_ref[:, cols])
        a_scr[:, cols] = (_silu(g) * u).astype(BF16)
    out = x + 0.5 * _dot(a_scr[...], wd_ref[...])
    if post_norm:
        out = _rms(out, fw_ref[...])
    o_ref[...] = out


def _ffn(x, nw, wg, wu, wd, *, proj=None, final_w=None, tm, tf=256):
    t, d = x.shape
    d_ff = wg.shape[1]
    row = lambda w: pl.BlockSpec((tm, w), lambda i: (i, 0))
    args, specs = [x], [row(d)]
    if proj is not None:
        oa, od, woa, wod = proj
        args += [oa, od, woa, wod]
        specs += [row(oa.shape[1]), row(od.shape[1]), _resident(woa.shape), _resident(wod.shape)]
    args += [nw, wg, wu, wd]
    specs += [_resident(nw.shape), _resident(wg.shape), _resident(wu.shape), _resident(wd.shape)]
    if final_w is not None:
        args.append(final_w)
        specs.append(_resident(final_w.shape))
    body = functools.partial(_ffn_body, tf=tf, pre_proj=proj is not None, post_norm=final_w is not None)
    return pl.pallas_call(
        body,
        grid=(t // tm,),
        in_specs=specs,
        out_specs=row(d),
        out_shape=jax.ShapeDtypeStruct((t, d), F32),
        scratch_shapes=[pltpu.VMEM((tm, d_ff), BF16)],
        compiler_params=pltpu.CompilerParams(dimension_semantics=("parallel",), vmem_limit_bytes=VMEM_LIMIT),
        name="ffn_proj" if proj is not None else "ffn",
    )(*args)


def _in_proj_body(x_ref, nw_ref, wqkv_ref, wgate_ref, wab_ref, wqd_ref, wkd_ref, wvd_ref,
                  cw_ref, alog_ref, dtb_ref, cos_ref, sin_ref,
                  qa_ref, ka_ref, va_ref, gate_ref, gcb_ref, gct_ref, qd_ref, kd_ref, vd_ref,
                  pbuf, *, blocks_per_seq):
    tm = x_ref.shape[0]
    h = _rms(x_ref[...], nw_ref[...]).astype(BF16)

    first = pl.program_id(0) % blocks_per_seq == 0
    for sec, out_ref in enumerate((qa_ref, ka_ref, va_ref)):
        cols = slice(sec * GROUP_W, (sec + 1) * GROUP_W)

        @pl.when(first)
        def _():
            pbuf[0:HALO, cols] = jnp.zeros((HALO, GROUP_W), F32)

        p = _dot(h, wqkv_ref[:, cols])
        pbuf[HALO:HALO + tm, cols] = p
        y = p * cw_ref[CONV_K - 1:CONV_K, cols]
        for j in range(CONV_K - 1):
            shift = CONV_K - 1 - j
            y = y + pbuf[HALO - shift:HALO - shift + tm, cols] * cw_ref[j:j + 1, cols]
        pbuf[0:HALO, cols] = pbuf[tm:tm + HALO, cols]
        y = _silu(y)
        if sec < 2:
            scale = HEAD_W ** -0.5 if sec == 0 else 1.0
            for hd in range(N_HEADS):
                hc = slice(hd * HEAD_W, (hd + 1) * HEAD_W)
                yh = y[:, hc]
                yh = yh * (lax.rsqrt(jnp.sum(yh * yh, axis=-1, keepdims=True) + EPS) * scale)
                out_ref[:, hc] = yh.astype(BF16)
        else:
            out_ref[...] = y.astype(BF16)

    gate_ref[...] = _dot(h, wgate_ref[...]).astype(BF16)

    ab = _dot(h, wab_ref[...])
    z = ab + dtb_ref[...]
    softplus = jnp.maximum(z, 0.0) + jnp.log1p(jnp.exp(-jnp.abs(z)))
    lane = lax.broadcasted_iota(jnp.int32, ab.shape, 1)
    g = jnp.where(lane < N_HEADS, -jnp.exp(alog_ref[...]) * softplus, 0.0)
    shift = int(math.log2(GDN_CHUNK))
    r = lax.broadcasted_iota(jnp.int32, (tm, tm), 0)
    cidx = lax.broadcasted_iota(jnp.int32, (tm, tm), 1)
    tri = jnp.where(((r >> shift) == (cidx >> shift)) & (cidx <= r), 1.0, 0.0).astype(BF16)
    gc = sum(_dot(tri, part) for part in _split3(g))
    gcb_ref[...] = jnp.where(lane < N_HEADS, gc, _sigmoid(ab))
    for cc in range(tm // GDN_CHUNK):
        blk = slice(cc * GDN_CHUNK, (cc + 1) * GDN_CHUNK)
        gct_ref[:, blk] = gc[blk, :].T[0:HALO, :]

    cos = cos_ref[...]
    sin = sin_ref[...]
    lane = lax.broadcasted_iota(jnp.int32, (tm, HEAD_W), 1)
    low_half = (lane % DIFF_D) < (DIFF_D // 2)
    for w_ref, out_ref, scale in ((wqd_ref, qd_ref, DIFF_D ** -0.5), (wkd_ref, kd_ref, 1.0)):
        p = _dot(h, w_ref[...])
        for hd in range(N_HEADS):
            hc = slice(hd * HEAD_W, (hd + 1) * HEAD_W)
            ph = p[:, hc]
            partner = jnp.where(low_half,
                                pltpu.roll(ph, HEAD_W - DIFF_D // 2, 1),
                                pltpu.roll(ph, DIFF_D // 2, 1))
            out_ref[:, hc] = ((ph * cos + partner * sin) * scale).astype(BF16)
    vd_ref[...] = _dot(h, wvd_ref[...]).astype(BF16)


def _in_proj(x, nw, wqkv, wgate, wab, wqd, wkd, wvd, cw, alog, dtb, cos, sin, *, seq, tm):
    t, d = x.shape
    blocks_per_seq = seq // tm
    row = lambda w: pl.BlockSpec((tm, w), lambda i: (i, 0))
    tab = pl.BlockSpec((tm, HEAD_W), lambda i: (i % blocks_per_seq, 0))
    res = [nw, wqkv, wgate, wab, wqd, wkd, wvd, cw, alog, dtb]
    bf = lambda: jax.ShapeDtypeStruct((t, GROUP_W), BF16)
    return pl.pallas_call(
        functools.partial(_in_proj_body, blocks_per_seq=blocks_per_seq),
        grid=(t // tm,),
        in_specs=[row(d)] + [_resident(a.shape) for a in res] + [tab, tab],
        out_specs=[row(GROUP_W)] * 4 + [row(HEAD_W), pl.BlockSpec((HALO, tm), lambda i: (0, i))] + [row(GROUP_W)] * 3,
        out_shape=[bf(), bf(), bf(), bf(), jax.ShapeDtypeStruct((t, HEAD_W), F32),
                   jax.ShapeDtypeStruct((HALO, t), F32), bf(), bf(), bf()],
        scratch_shapes=[pltpu.VMEM((tm + HALO, 3 * GROUP_W), F32)],
        compiler_params=pltpu.CompilerParams(dimension_semantics=("arbitrary",), vmem_limit_bytes=VMEM_LIMIT),
        name="in_proj",
    )(x, *res, cos, sin)


def _split3(x):
    hi = x.astype(BF16)
    r = x - hi.astype(F32)
    mid = r.astype(BF16)
    lo = (r - mid.astype(F32)).astype(BF16)
    return hi, mid, lo


def _block_diag(x):
    half = x.shape[1] // 2
    z = jnp.zeros((x.shape[0], half), x.dtype)
    return jnp.concatenate([jnp.concatenate([x[:, :half], z], axis=1),
                            jnp.concatenate([z, x[:, half:]], axis=1)], axis=0)


def _gdn_body(q_ref, k_ref, v_ref, gate_ref, gcb_ref, gct_ref, nw_ref, o_ref,
              u_scr, w_scr, qd_scr, qk_scr, kdt_scr, erow_scr, state, *, group):
    c = GDN_CHUNK
    pw = 2 * HEAD_W
    n_pairs = N_HEADS // 2
    n_chunks = q_ref.shape[0] // c
    levels = int(math.log2(c))
    row = lax.broadcasted_iota(jnp.int32, (c, pw), 0)
    col = lax.broadcasted_iota(jnp.int32, (c, pw), 1) & (HEAD_W - 1)
    incl = row >= col
    strict = row > col
    eye = jnp.where(row == col, 1.0, 0.0)
    xor = row ^ col
    pair_level = jnp.full((c, pw), -1, jnp.int32)
    for level in range(levels):
        pair_level = jnp.where((xor >> level) == 1, level, pair_level)

    def bcast_pair(arr, lane0, lane1):
        return jnp.concatenate([jnp.broadcast_to(arr[:, lane0:lane0 + 1], (c, HEAD_W)),
                                jnp.broadcast_to(arr[:, lane1:lane1 + 1], (c, HEAD_W))], axis=1)

    def prepare(gi, carry):
        chains = []
        for j in range(group):
            r0 = pl.multiple_of((gi * group + j) * c, c)
            rows = pl.ds(r0, c)
            gcb = gcb_ref[rows, :]
            g_last = gcb[c - 1:c, :]
            e_gc = jnp.exp(gcb)
            e_rest = jnp.exp(g_last - gcb)
            e_last = jnp.exp(g_last)
            e_rows = pl.ds(pl.multiple_of((gi * group + j) * HALO, HALO), HALO)
            erow_scr[e_rows, :] = jnp.concatenate(
                [jnp.broadcast_to(e_last[:, hd:hd + 1], (HALO, HEAD_W)) for hd in range(N_HEADS)], axis=1)
            for p in range(n_pairs):
                h0, h1 = 2 * p, 2 * p + 1
                cols = slice(p * pw, (p + 1) * pw)
                q = q_ref[rows, cols]
                k = k_ref[rows, cols]
                kf = k.astype(F32)
                beta = bcast_pair(gcb, N_HEADS + h0, N_HEADS + h1)
                e_gc_p = bcast_pair(e_gc, h0, h1)
                gc_col = bcast_pair(gcb, h0, h1)
                gc_row = jnp.concatenate([jnp.broadcast_to(gct_ref[h0:h0 + 1, rows], (c, HEAD_W)),
                                          jnp.broadcast_to(gct_ref[h1:h1 + 1, rows], (c, HEAD_W))], axis=1)
                decay = jnp.where(incl, jnp.exp(jnp.minimum(gc_col - gc_row, 0.0)), 0.0)
                kb = kf * beta
                prod = _dot_nt(jnp.concatenate([kb.astype(BF16), q], axis=0), _block_diag(k))
                m = jnp.where(strict, prod[:c] * decay, 0.0)
                qk_scr[rows, cols] = (prod[c:] * decay).astype(BF16)
                qd_scr[rows, cols] = (q.astype(F32) * e_gc_p).astype(BF16)
                kd = kf * bcast_pair(e_rest, h0, h1)
                kdt_scr[rows, cols] = jnp.concatenate([kd[:, :HEAD_W].T, kd[:, HEAD_W:].T], axis=1).astype(BF16)
                chains.append(dict(
                    rows=rows, cols=cols, m=m.astype(BF16),
                    vb=(v_ref[rows, cols].astype(F32) * beta).astype(BF16),
                    kbe=(kb * e_gc_p).astype(BF16),
                    t=eye - jnp.where(pair_level == 0, m, 0.0)))
        for level in range(1, levels):
            for ch in chains:
                ch["t_bf"] = ch["t"].astype(BF16)
                ch["x"] = _dot(ch["m"], _block_diag(ch["t_bf"])).astype(BF16)
            for ch in chains:
                y = _dot(ch["t_bf"], _block_diag(ch["x"]))
                ch["t"] = ch["t"] - jnp.where(pair_level == level, y, 0.0)
        for ch in chains:
            t_bf = ch["t"].astype(BF16)
            u_scr[ch["rows"], ch["cols"]] = _dot(t_bf, _block_diag(ch["vb"]))
            w_scr[ch["rows"], ch["cols"]] = _dot(t_bf, _block_diag(ch["kbe"])).astype(BF16)
        return carry

    lax.fori_loop(0, n_chunks // group, prepare, 0)

    nw = jnp.concatenate([nw_ref[...], nw_ref[...]], axis=1)
    state[...] = jnp.zeros(state.shape, F32)

    def recur(i, carry):
        rows = pl.ds(pl.multiple_of(i * c, c), c)
        e_row = erow_scr[pl.ds(pl.multiple_of(i * HALO, HALO), 1), :]
        for p in range(n_pairs):
            cols = slice(p * pw, (p + 1) * pw)
            s = state[p]
            r1 = _dot(jnp.concatenate([w_scr[rows, cols], qd_scr[rows, cols]], axis=0),
                      _block_diag(s.astype(BF16)))
            v_new = (u_scr[rows, cols] - r1[:c]).astype(BF16)
            r2 = _dot(jnp.concatenate([kdt_scr[rows, cols], qk_scr[rows, cols]], axis=0),
                      _block_diag(v_new))
            state[p] = s * e_row[:, cols] + r2[:c]
            o = r1[c:] + r2[c:]
            ms = jnp.concatenate(
                [jnp.broadcast_to(jnp.mean(o[:, hh * HEAD_W:(hh + 1) * HEAD_W] ** 2, axis=-1, keepdims=True),
                                  (c, HEAD_W)) for hh in range(2)], axis=1)
            o = o * lax.rsqrt(ms + EPS) * nw * _silu(gate_ref[rows, cols].astype(F32))
            o_ref[rows, cols] = o.astype(BF16)
        return carry

    lax.fori_loop(0, n_chunks, recur, 0)


def _gdn(q, k, v, gate, gcb, gct, nw, *, seq, group=2):
    t = q.shape[0]
    n_chunks = seq // GDN_CHUNK
    group = math.gcd(group, n_chunks)
    seq_blk = lambda w: pl.BlockSpec((seq, w), lambda b: (b, 0))
    return pl.pallas_call(
        functools.partial(_gdn_body, group=group),
        grid=(t // seq,),
        in_specs=[seq_blk(GROUP_W)] * 4 + [seq_blk(HEAD_W), pl.BlockSpec((HALO, seq), lambda b: (0, b)),
                                           _resident(nw.shape)],
        out_specs=seq_blk(GROUP_W),
        out_shape=jax.ShapeDtypeStruct((t, GROUP_W), BF16),
        scratch_shapes=[pltpu.VMEM((seq, GROUP_W), F32)] + [pltpu.VMEM((seq, GROUP_W), BF16)] * 4
                       + [pltpu.VMEM((n_chunks * HALO, GROUP_W), F32),
                          pltpu.VMEM((N_HEADS // 2, GDN_CHUNK, 2 * HEAD_W), F32)],
        compiler_params=pltpu.CompilerParams(dimension_semantics=("parallel",), vmem_limit_bytes=VMEM_LIMIT),
        name="gdn",
    )(q, k, v, gate, gcb, gct, nw)


def _diff_body(q_ref, k_ref, v_ref, lq1_ref, lk1_ref, lq2_ref, lk2_ref, nw_ref, o_ref, *, lam_init):
    tq = q_ref.shape[0]
    i = pl.program_id(2)
    q = q_ref[...]
    lane = lax.broadcasted_iota(jnp.int32, q.shape, 1)
    zero = jnp.zeros_like(q)
    q2 = jnp.concatenate([jnp.where(lane < DIFF_D, q, zero), jnp.where(lane >= DIFF_D, q, zero)], axis=0)

    def step(rows, masked, carry):
        m, l, acc = carry
        s = _dot_nt(q2, k_ref[rows, :])
        if masked:
            r = lax.broadcasted_iota(jnp.int32, (tq, tq), 0)
            cidx = lax.broadcasted_iota(jnp.int32, (tq, tq), 1)
            keep = jnp.concatenate([cidx <= r, cidx <= r], axis=0)
            s = jnp.where(keep, s, -jnp.inf)
        m_new = jnp.maximum(m, jnp.max(s, axis=-1, keepdims=True))
        alpha = jnp.exp(m - m_new)
        p = jnp.exp(s - m_new)
        l = alpha * l + jnp.sum(p, axis=-1, keepdims=True)
        acc = alpha * acc + _dot(p.astype(BF16), v_ref[rows, :])
        return m_new, l, acc

    init = (jnp.full((2 * tq, 1), -jnp.inf, F32), jnp.zeros((2 * tq, 1), F32), jnp.zeros((2 * tq, HEAD_W), F32))
    carry = lax.fori_loop(
        0, i, lambda j, cr: step(pl.ds(pl.multiple_of(j * tq, tq), tq), False, cr), init)
    _, l, acc = step(pl.ds(pl.multiple_of(i * tq, tq), tq), True, carry)

    lam = (jnp.exp(jnp.sum(lq1_ref[...] * lk1_ref[...], axis=-1, keepdims=True))
           - jnp.exp(jnp.sum(lq2_ref[...] * lk2_ref[...], axis=-1, keepdims=True)) + lam_init)
    o = acc / l
    o = o[:tq] - lam * o[tq:]
    o_ref[...] = (_rms(o, nw_ref[...]) * (1.0 - lam_init)).astype(BF16)


def _diff_attn(q, k, v, lq1, lk1, lq2, lk2, nw, *, seq, lam_init, tq):
    t = q.shape[0]
    nq = seq // tq
    q_blk = pl.BlockSpec((tq, HEAD_W), lambda b, h, i: (b * nq + i, h))
    kv_blk = pl.BlockSpec((seq, HEAD_W), lambda b, h, i: (b, h))
    small = [lq1, lk1, lq2, lk2, nw]
    return pl.pallas_call(
        functools.partial(_diff_body, lam_init=lam_init),
        grid=(t // seq, N_HEADS, nq),
        in_specs=[q_blk, kv_blk, kv_blk] + [_resident(a.shape) for a in small],
        out_specs=q_blk,
        out_shape=jax.ShapeDtypeStruct((t, GROUP_W), BF16),
        compiler_params=pltpu.CompilerParams(
            dimension_semantics=("parallel", "parallel", "arbitrary"), vmem_limit_bytes=VMEM_LIMIT),
        name="diff_attn",
    )(q, k, v, *small)


def _rope_tables(seq):
    inv = 1.0 / (ROPE_THETA ** (jnp.arange(0, DIFF_D, 2, dtype=F32) / DIFF_D))
    ang = jnp.arange(seq, dtype=F32)[:, None] * inv[None, :]
    cos, sin = jnp.cos(ang), jnp.sin(ang)
    reps = HEAD_W // DIFF_D
    return (jnp.tile(jnp.concatenate([cos, cos], axis=-1), (1, reps)),
            jnp.tile(jnp.concatenate([-sin, sin], axis=-1), (1, reps)))


def _lambda_init(layer):
    return 0.8 - 0.6 * math.exp(-0.3 * layer)


def kernel(x, ffn1_norm, ffn1_w_gate, ffn1_w_up, ffn1_w_down, mix_norm, w_in, conv_w, a_log, dt_bias, gdn_norm, lambda_q1, lambda_k1, lambda_q2, lambda_k2, diff_norm, w_out, ffn2_norm, ffn2_w_gate, ffn2_w_up, ffn2_w_down, final_norm):
    b, seq, d = x.shape
    depth = w_in.shape[0]
    t = b * seq
    tm = min(512, seq)
    tq = min(256, seq)
    assert seq % GDN_CHUNK == 0 and seq % tm == 0 and seq % tq == 0
    n_qkv = 3 * GROUP_W
    bounds = [0, n_qkv, n_qkv + GROUP_W, n_qkv + GROUP_W + N_HEADS, n_qkv + GROUP_W + 2 * N_HEADS]
    bounds += [bounds[-1] + GROUP_W, bounds[-1] + 2 * GROUP_W, bounds[-1] + 3 * GROUP_W]
    assert bounds[-1] == w_in.shape[2]
    cos, sin = _rope_tables(seq)
    row = lambda a: a.reshape(1, -1).astype(F32)
    pad_lanes = lambda a: jnp.pad(a, ((0, 0), (0, HEAD_W - a.shape[1])))
    bf = lambda a: a.astype(BF16)

    xf = x.reshape(t, d).astype(F32)
    for l in range(depth):
        if l == 0:
            xf = _ffn(xf, row(ffn1_norm[l]), bf(ffn1_w_gate[l]), bf(ffn1_w_up[l]), bf(ffn1_w_down[l]), tm=tm)
        wl = w_in[l]
        piece = lambda n: wl[:, bounds[n]:bounds[n + 1]]
        w_ab = pad_lanes(jnp.concatenate([piece(2), piece(3)], axis=1))
        qa, ka, va, gate, gcb, gct, qd, kd, vd = _in_proj(
            xf, row(mix_norm[l]), bf(piece(0)), bf(piece(1)), bf(w_ab), bf(piece(4)), bf(piece(5)), bf(piece(6)),
            conv_w[l].astype(F32), pad_lanes(row(a_log[l])), pad_lanes(row(dt_bias[l])), cos, sin, seq=seq, tm=tm)
        o_a = _gdn(qa, ka, va, gate, gcb, gct, row(gdn_norm[l]), seq=seq)
        o_d = _diff_attn(qd, kd, vd, row(lambda_q1[l]), row(lambda_k1[l]), row(lambda_q2[l]), row(lambda_k2[l]),
                         row(diff_norm[l]), seq=seq, lam_init=_lambda_init(l), tq=tq)
        wo = w_out[l]
        last = l == depth - 1
        xf = _ffn(xf, row(ffn2_norm[l]), bf(ffn2_w_gate[l]), bf(ffn2_w_up[l]), bf(ffn2_w_down[l]),
                  proj=(o_a, o_d, bf(wo[:GROUP_W]), bf(wo[GROUP_W:])),
                  final_w=row(final_norm) if last else None, tm=tm)
        if not last:
            xf = _ffn(xf, row(ffn1_norm[l + 1]), bf(ffn1_w_gate[l + 1]), bf(ffn1_w_up[l + 1]),
                      bf(ffn1_w_down[l + 1]), tm=tm)
    return xf.reshape(b, seq, d).astype(x.dtype)
```

```python
import functools
import math

import jax
import jax.numpy as jnp
from jax import lax
from jax.experimental import pallas as pl
from jax.experimental.pallas import tpu as pltpu

F32 = jnp.float32
BF16 = jnp.bfloat16

EPS = 1e-6
ROPE_THETA = 10000.0
HEAD_W = 128
N_HEADS = 4
GROUP_W = N_HEADS * HEAD_W
DIFF_D = 64
LOG2_E = math.log2(math.e)
CONV_K = 4
HALO = 8
GDN_CHUNK = 128
VMEM_LIMIT = 56 * 1024 * 1024


def _dot(a, b):
    return jnp.dot(a, b, preferred_element_type=F32)


def _dot_nt(a, b):
    return lax.dot_general(a, b, (((1,), (1,)), ((), ())), preferred_element_type=F32)


def _rms(x, w):
    return x * lax.rsqrt(jnp.mean(x * x, axis=-1, keepdims=True) + EPS) * w


def _sigmoid(x):
    return 1.0 / (1.0 + jnp.exp(-x))


def _silu(x):
    return x * _sigmoid(x)


def _resident(shape):
    return pl.BlockSpec(shape, lambda *_: (0,) * len(shape), pipeline_mode=pl.Buffered(1))


def _ffn_body(*refs, tf, pre_proj, post_norm):
    it = iter(refs)
    x_ref = next(it)
    if pre_proj:
        oa_ref, od_ref, woa_ref, wod_ref = next(it), next(it), next(it), next(it)
    nw_ref, wg_ref, wu_ref, wd_ref = next(it), next(it), next(it), next(it)
    if post_norm:
        fw_ref = next(it)
    o_ref, a_scr = next(it), next(it)

    x = x_ref[...]
    if pre_proj:
        x = x + _dot(oa_ref[...], woa_ref[...]) + _dot(od_ref[...], wod_ref[...])
    h = _rms(x, nw_ref[...]).astype(BF16)
    d_ff = wg_ref.shape[1]
    for c in range(d_ff // tf):
        cols = slice(c * tf, (c + 1) * tf)
        g = _dot(h, wg_ref[:, cols])
        u = _dot(h, wu_ref[:, cols])
        a_scr[:, cols] = (_silu(g) * u).astype(BF16)
    out = x + 0.5 * _dot(a_scr[...], wd_ref[...])
    if post_norm:
        out = _rms(out, fw_ref[...])
    o_ref[...] = out


def _ffn(x, nw, wg, wu, wd, *, proj=None, final_w=None, tm, tf=256):
    t, d = x.shape
    d_ff = wg.shape[1]
    row = lambda w: pl.BlockSpec((tm, w), lambda i: (i, 0))
    args, specs = [x], [row(d)]
    if proj is not None:
        oa, od, woa, wod = proj
        args += [oa, od, woa, wod]
        specs += [row(oa.shape[1]), row(od.shape[1]), _resident(woa.shape), _resident(wod.shape)]
    args += [nw, wg, wu, wd]
    specs += [_resident(nw.shape), _resident(wg.shape), _resident(wu.shape), _resident(wd.shape)]
    if final_w is not None:
        args.append(final_w)
        specs.append(_resident(final_w.shape))
    body = functools.partial(_ffn_body, tf=tf, pre_proj=proj is not None, post_norm=final_w is not None)
    return pl.pallas_call(
        body,
        grid=(t // tm,),
        in_specs=specs,
        out_specs=row(d),
        out_shape=jax.ShapeDtypeStruct((t, d), F32),
        scratch_shapes=[pltpu.VMEM((tm, d_ff), BF16)],
        compiler_params=pltpu.CompilerParams(dimension_semantics=("parallel",), vmem_limit_bytes=VMEM_LIMIT),
        name="ffn_proj" if proj is not None else "ffn",
    )(*args)


def _in_proj_body(x_ref, nw_ref, wqkv_ref, wgate_ref, wab_ref, wqd_ref, wkd_ref, wvd_ref,
                  cw_ref, alog_ref, dtb_ref, cos_ref, sin_ref,
                  qa_ref, ka_ref, va_ref, gate_ref, gcb_ref, gct_ref, qd_ref, kd_ref, vd_ref,
                  pbuf, *, blocks_per_seq):
    tm = x_ref.shape[0]

    @pl.when(pl.program_id(0) % blocks_per_seq == 0)
    def _():
        pbuf[0:HALO, :] = jnp.zeros((HALO, pbuf.shape[1]), F32)

    h = _rms(x_ref[...], nw_ref[...]).astype(BF16)

    def conv_section(sec, out_ref):
        cols = slice(sec * GROUP_W, (sec + 1) * GROUP_W)
        p = _dot(h, wqkv_ref[:, cols])
        pbuf[HALO:HALO + tm, cols] = p
        y = p * cw_ref[CONV_K - 1:CONV_K, cols]
        for j in range(CONV_K - 1):
            shift = CONV_K - 1 - j
            y = y + pbuf[HALO - shift:HALO - shift + tm, cols] * cw_ref[j:j + 1, cols]
        pbuf[0:HALO, cols] = pbuf[tm:tm + HALO, cols]
        y = _silu(y)
        if sec < 2:
            scale = HEAD_W ** -0.5 if sec == 0 else 1.0
            for hd in range(N_HEADS):
                hc = slice(hd * HEAD_W, (hd + 1) * HEAD_W)
                yh = y[:, hc]
                yh = yh * (lax.rsqrt(jnp.sum(yh * yh, axis=-1, keepdims=True) + EPS) * scale)
                out_ref[:, hc] = yh.astype(BF16)
        else:
            out_ref[...] = y.astype(BF16)

    def rope_section(w_ref, out_ref, scale):
        cos = cos_ref[...]
        sin = sin_ref[...]
        low_half = (lax.broadcasted_iota(jnp.int32, (tm, HEAD_W), 1) % DIFF_D) < (DIFF_D // 2)
        p = _dot(h, w_ref[...])
        for hd in range(N_HEADS):
            hc = slice(hd * HEAD_W, (hd + 1) * HEAD_W)
            ph = p[:, hc]
            partner = jnp.where(low_half,
                                pltpu.roll(ph, HEAD_W - DIFF_D // 2, 1),
                                pltpu.roll(ph, DIFF_D // 2, 1))
            out_ref[:, hc] = ((ph * cos + partner * sin) * scale).astype(BF16)

    conv_section(0, qa_ref)
    conv_section(1, ka_ref)
    conv_section(2, va_ref)
    gate_ref[...] = _dot(h, wgate_ref[...]).astype(BF16)

    ab = _dot(h, wab_ref[...])
    z = ab + dtb_ref[...]
    softplus = jnp.maximum(z, 0.0) + jnp.log1p(jnp.exp(-jnp.abs(z)))
    lane = lax.broadcasted_iota(jnp.int32, ab.shape, 1)
    g = jnp.where(lane < N_HEADS, -jnp.exp(alog_ref[...]) * softplus, 0.0)
    shift = int(math.log2(GDN_CHUNK))
    r = lax.broadcasted_iota(jnp.int32, (tm, tm), 0)
    cidx = lax.broadcasted_iota(jnp.int32, (tm, tm), 1)
    tri = jnp.where(((r >> shift) == (cidx >> shift)) & (cidx <= r), 1.0, 0.0).astype(BF16)
    gc = sum(_dot(tri, part) for part in _split3(g))
    gcb_ref[...] = jnp.where(lane < N_HEADS, gc, _sigmoid(ab))
    for cc in range(tm // GDN_CHUNK):
        blk = slice(cc * GDN_CHUNK, (cc + 1) * GDN_CHUNK)
        gct_ref[:, blk] = gc[blk, :].T[0:HALO, :]

    rope_section(wqd_ref, qd_ref, DIFF_D ** -0.5 * LOG2_E)
    rope_section(wkd_ref, kd_ref, 1.0)
    vd_ref[...] = _dot(h, wvd_ref[...]).astype(BF16)


def _in_proj(x, nw, wqkv, wgate, wab, wqd, wkd, wvd, cw, alog, dtb, cos, sin, *, seq, tm):
    t, d = x.shape
    blocks_per_seq = seq // tm
    row = lambda w: pl.BlockSpec((tm, w), lambda i: (i, 0))
    tab = pl.BlockSpec((tm, HEAD_W), lambda i: (i % blocks_per_seq, 0))
    res = [nw, wqkv, wgate, wab, wqd, wkd, wvd, cw, alog, dtb]
    bf = lambda: jax.ShapeDtypeStruct((t, GROUP_W), BF16)
    return pl.pallas_call(
        functools.partial(_in_proj_body, blocks_per_seq=blocks_per_seq),
        grid=(t // tm,),
        in_specs=[row(d)] + [_resident(a.shape) for a in res] + [tab, tab],
        out_specs=[row(GROUP_W)] * 4 + [row(HEAD_W), pl.BlockSpec((HALO, tm), lambda i: (0, i))] + [row(GROUP_W)] * 3,
        out_shape=[bf(), bf(), bf(), bf(), jax.ShapeDtypeStruct((t, HEAD_W), F32),
                   jax.ShapeDtypeStruct((HALO, t), F32), bf(), bf(), bf()],
        scratch_shapes=[pltpu.VMEM((tm + HALO, 3 * GROUP_W), F32)],
        compiler_params=pltpu.CompilerParams(dimension_semantics=("arbitrary",), vmem_limit_bytes=VMEM_LIMIT),
        name="in_proj",
    )(x, *res, cos, sin)


def _split3(x):
    hi = x.astype(BF16)
    r = x - hi.astype(F32)
    mid = r.astype(BF16)
    lo = (r - mid.astype(F32)).astype(BF16)
    return hi, mid, lo


def _block_diag(x):
    half = x.shape[1] // 2
    z = jnp.zeros((x.shape[0], half), x.dtype)
    return jnp.concatenate([jnp.concatenate([x[:, :half], z], axis=1),
                            jnp.concatenate([z, x[:, half:]], axis=1)], axis=0)


def _gdn_body(q_ref, k_ref, v_ref, gate_ref, gcb_ref, gct_ref, nw_ref, o_ref,
              u_scr, w_scr, qd_scr, qk_scr, kdt_scr, erow_scr, state, *, group):
    c = GDN_CHUNK
    pw = 2 * HEAD_W
    n_pairs = N_HEADS // 2
    n_chunks = q_ref.shape[0] // c
    levels = int(math.log2(c))
    row = lax.broadcasted_iota(jnp.int32, (c, pw), 0)
    col = lax.broadcasted_iota(jnp.int32, (c, pw), 1) & (HEAD_W - 1)
    incl = row >= col
    strict = row > col
    eye = jnp.where(row == col, 1.0, 0.0)
    xor = row ^ col
    pair_level = jnp.full((c, pw), -1, jnp.int32)
    for level in range(levels):
        pair_level = jnp.where((xor >> level) == 1, level, pair_level)

    def bcast_pair(arr, lane0, lane1):
        return jnp.concatenate([jnp.broadcast_to(arr[:, lane0:lane0 + 1], (c, HEAD_W)),
                                jnp.broadcast_to(arr[:, lane1:lane1 + 1], (c, HEAD_W))], axis=1)

    def prepare(gi):
        chains = []
        for j in range(group):
            rows = slice((gi * group + j) * c, (gi * group + j + 1) * c)
            gcb = gcb_ref[rows, :]
            g_last = gcb[c - 1:c, :]
            e_gc = jnp.exp(gcb)
            e_rest = jnp.exp(g_last - gcb)
            e_last = jnp.exp(g_last)
            e_rows = slice((gi * group + j) * HALO, (gi * group + j + 1) * HALO)
            erow_scr[e_rows, :] = jnp.concatenate(
                [jnp.broadcast_to(e_last[:, hd:hd + 1], (HALO, HEAD_W)) for hd in range(N_HEADS)], axis=1)
            for p in range(n_pairs):
                h0, h1 = 2 * p, 2 * p + 1
                cols = slice(p * pw, (p + 1) * pw)
                q = q_ref[rows, cols]
                k = k_ref[rows, cols]
                kf = k.astype(F32)
                beta = bcast_pair(gcb, N_HEADS + h0, N_HEADS + h1)
                e_gc_p = bcast_pair(e_gc, h0, h1)
                gc_col = bcast_pair(gcb, h0, h1)
                gc_row = jnp.concatenate([jnp.broadcast_to(gct_ref[h0:h0 + 1, rows], (c, HEAD_W)),
                                          jnp.broadcast_to(gct_ref[h1:h1 + 1, rows], (c, HEAD_W))], axis=1)
                decay = jnp.where(incl, jnp.exp(jnp.minimum(gc_col - gc_row, 0.0)), 0.0)
                kb = kf * beta
                prod = _dot_nt(jnp.concatenate([kb.astype(BF16), q], axis=0), _block_diag(k))
                m = jnp.where(strict, prod[:c] * decay, 0.0)
                qk_scr[rows, cols] = (prod[c:] * decay).astype(BF16)
                qd_scr[rows, cols] = (q.astype(F32) * e_gc_p).astype(BF16)
                kd = kf * bcast_pair(e_rest, h0, h1)
                kdt_scr[rows, cols] = jnp.concatenate([kd[:, :HEAD_W].T, kd[:, HEAD_W:].T], axis=1).astype(BF16)
                chains.append(dict(
                    rows=rows, cols=cols, m=m.astype(BF16),
                    vb=(v_ref[rows, cols].astype(F32) * beta).astype(BF16),
                    kbe=(kb * e_gc_p).astype(BF16),
                    t=eye - jnp.where(pair_level == 0, m, 0.0)))
        for level in range(1, levels):
            for ch in chains:
                ch["t_bf"] = ch["t"].astype(BF16)
                ch["x"] = _dot(ch["m"], _block_diag(ch["t_bf"])).astype(BF16)
            for ch in chains:
                y = _dot(ch["t_bf"], _block_diag(ch["x"]))
                ch["t"] = ch["t"] - jnp.where(pair_level == level, y, 0.0)
        for ch in chains:
            t_bf = ch["t"].astype(BF16)
            u_scr[ch["rows"], ch["cols"]] = _dot(t_bf, _block_diag(ch["vb"]))
            w_scr[ch["rows"], ch["cols"]] = _dot(t_bf, _block_diag(ch["kbe"])).astype(BF16)

    nw = jnp.concatenate([nw_ref[...], nw_ref[...]], axis=1)
    state[...] = jnp.zeros(state.shape, F32)

    def recur(i):
        rows = slice(i * c, (i + 1) * c)
        e_row = erow_scr[i * HALO:i * HALO + 1, :]
        for p in range(n_pairs):
            cols = slice(p * pw, (p + 1) * pw)
            s = state[p]
            r1 = _dot(jnp.concatenate([w_scr[rows, cols], qd_scr[rows, cols]], axis=0),
                      _block_diag(s.astype(BF16)))
            v_new = (u_scr[rows, cols] - r1[:c]).astype(BF16)
            r2 = _dot(jnp.concatenate([kdt_scr[rows, cols], qk_scr[rows, cols]], axis=0),
                      _block_diag(v_new))
            state[p] = s * e_row[:, cols] + r2[:c]
            o = r1[c:] + r2[c:]
            ms = jnp.concatenate(
                [jnp.broadcast_to(jnp.mean(o[:, hh * HEAD_W:(hh + 1) * HEAD_W] ** 2, axis=-1, keepdims=True),
                                  (c, HEAD_W)) for hh in range(2)], axis=1)
            o = o * lax.rsqrt(ms + EPS) * nw * _silu(gate_ref[rows, cols].astype(F32))
            o_ref[rows, cols] = o.astype(BF16)

    n_groups = n_chunks // group
    for gi in range(n_groups + 1):
        if gi < n_groups:
            prepare(gi)
        if gi > 0:
            for i in range((gi - 1) * group, gi * group):
                recur(i)


def _gdn(q, k, v, gate, gcb, gct, nw, *, seq, group=4):
    t = q.shape[0]
    n_chunks = seq // GDN_CHUNK
    group = math.gcd(group, n_chunks)
    seq_blk = lambda w: pl.BlockSpec((seq, w), lambda b: (b, 0))
    return pl.pallas_call(
        functools.partial(_gdn_body, group=group),
        grid=(t // seq,),
        in_specs=[seq_blk(GROUP_W)] * 4 + [seq_blk(HEAD_W), pl.BlockSpec((HALO, seq), lambda b: (0, b)),
                                           _resident(nw.shape)],
        out_specs=seq_blk(GROUP_W),
        out_shape=jax.ShapeDtypeStruct((t, GROUP_W), BF16),
        scratch_shapes=[pltpu.VMEM((seq, GROUP_W), F32)] + [pltpu.VMEM((seq, GROUP_W), BF16)] * 4
                       + [pltpu.VMEM((n_chunks * HALO, GROUP_W), F32),
                          pltpu.VMEM((N_HEADS // 2, GDN_CHUNK, 2 * HEAD_W), F32)],
        compiler_params=pltpu.CompilerParams(dimension_semantics=("parallel",), vmem_limit_bytes=VMEM_LIMIT),
        name="gdn",
    )(q, k, v, gate, gcb, gct, nw)


def _diff_body(q_ref, k_ref, v_ref, lq1_ref, lk1_ref, lq2_ref, lk2_ref, nw_ref, o_ref,
               v1_scr, *, lam_init, tq):
    seq = q_ref.shape[0]
    rows2 = 2 * tq
    lane = lax.broadcasted_iota(jnp.int32, (tq, HEAD_W), 1)
    v1_scr[:, :HEAD_W] = v_ref[...]
    v1_scr[:, HEAD_W:] = jnp.where(lax.broadcasted_iota(jnp.int32, (seq, HEAD_W), 1) == 0, 1.0, 0.0).astype(BF16)
    r = lax.broadcasted_iota(jnp.int32, (rows2, tq), 0) & (tq - 1)
    keep = lax.broadcasted_iota(jnp.int32, (rows2, tq), 1) <= r
    lam = (jnp.exp(jnp.sum(lq1_ref[...] * lk1_ref[...], axis=-1, keepdims=True))
           - jnp.exp(jnp.sum(lq2_ref[...] * lk2_ref[...], axis=-1, keepdims=True)) + lam_init)
    nw = nw_ref[...]

    for i in range(seq // tq):
        q_rows = slice(i * tq, (i + 1) * tq)
        n_keys = (i + 1) * tq
        q = q_ref[q_rows, :]
        zero = jnp.zeros_like(q)
        q2 = jnp.concatenate([jnp.where(lane < DIFF_D, q, zero), jnp.where(lane >= DIFF_D, q, zero)], axis=0)
        s = _dot_nt(q2, k_ref[0:n_keys, :])
        s_diag = jnp.where(keep, s[:, i * tq:], -jnp.inf)
        s = s_diag if i == 0 else jnp.concatenate([s[:, :i * tq], s_diag], axis=1)
        m = jnp.max(s, axis=-1, keepdims=True)
        p = jnp.exp2((s - m).astype(BF16))
        acc = _dot(p, v1_scr[0:n_keys, :])
        o = acc[:, :HEAD_W] / acc[:, HEAD_W:HEAD_W + 1]
        o = o[:tq] - lam * o[tq:]
        o_ref[q_rows, :] = (_rms(o, nw) * (1.0 - lam_init)).astype(BF16)


def _diff_attn(q, k, v, lq1, lk1, lq2, lk2, nw, *, seq, lam_init, tq):
    t = q.shape[0]
    blk = pl.BlockSpec((seq, HEAD_W), lambda b, h: (b, h))
    small = [lq1, lk1, lq2, lk2, nw]
    return pl.pallas_call(
        functools.partial(_diff_body, lam_init=lam_init, tq=tq),
        grid=(t // seq, N_HEADS),
        in_specs=[blk, blk, blk] + [_resident(a.shape) for a in small],
        out_specs=blk,
        out_shape=jax.ShapeDtypeStruct((t, GROUP_W), BF16),
        scratch_shapes=[pltpu.VMEM((seq, 2 * HEAD_W), BF16)],
        compiler_params=pltpu.CompilerParams(
            dimension_semantics=("parallel", "parallel"), vmem_limit_bytes=VMEM_LIMIT),
        name="diff_attn",
    )(q, k, v, *small)


def _rope_tables(seq):
    inv = 1.0 / (ROPE_THETA ** (jnp.arange(0, DIFF_D, 2, dtype=F32) / DIFF_D))
    ang = jnp.arange(seq, dtype=F32)[:, None] * inv[None, :]
    cos, sin = jnp.cos(ang), jnp.sin(ang)
    reps = HEAD_W // DIFF_D
    return (jnp.tile(jnp.concatenate([cos, cos], axis=-1), (1, reps)),
            jnp.tile(jnp.concatenate([-sin, sin], axis=-1), (1, reps)))


def _lambda_init(layer):
    return 0.8 - 0.6 * math.exp(-0.3 * layer)


def kernel(x, ffn1_norm, ffn1_w_gate, ffn1_w_up, ffn1_w_down, mix_norm, w_in, conv_w, a_log, dt_bias, gdn_norm, lambda_q1, lambda_k1, lambda_q2, lambda_k2, diff_norm, w_out, ffn2_norm, ffn2_w_gate, ffn2_w_up, ffn2_w_down, final_norm):
    b, seq, d = x.shape
    depth = w_in.shape[0]
    t = b * seq
    tm = min(512, seq)
    tq = min(256, seq)
    assert seq % GDN_CHUNK == 0 and seq % tm == 0 and seq % tq == 0
    n_qkv = 3 * GROUP_W
    bounds = [0, n_qkv, n_qkv + GROUP_W, n_qkv + GROUP_W + N_HEADS, n_qkv + GROUP_W + 2 * N_HEADS]
    bounds += [bounds[-1] + GROUP_W, bounds[-1] + 2 * GROUP_W, bounds[-1] + 3 * GROUP_W]
    assert bounds[-1] == w_in.shape[2]
    cos, sin = _rope_tables(seq)
    row = lambda a: a.reshape(1, -1).astype(F32)
    pad_lanes = lambda a: jnp.pad(a, ((0, 0), (0, HEAD_W - a.shape[1])))
    bf = lambda a: a.astype(BF16)

    xf = x.reshape(t, d).astype(F32)
    for l in range(depth):
        if l == 0:
            xf = _ffn(xf, row(ffn1_norm[l]), bf(ffn1_w_gate[l]), bf(ffn1_w_up[l]), bf(ffn1_w_down[l]), tm=tm)
        wl = w_in[l]
        piece = lambda n: wl[:, bounds[n]:bounds[n + 1]]
        w_ab = pad_lanes(jnp.concatenate([piece(2), piece(3)], axis=1))
        qa, ka, va, gate, gcb, gct, qd, kd, vd = _in_proj(
            xf, row(mix_norm[l]), bf(piece(0)), bf(piece(1)), bf(w_ab), bf(piece(4)), bf(piece(5)), bf(piece(6)),
            conv_w[l].astype(F32), pad_lanes(row(a_log[l])), pad_lanes(row(dt_bias[l])), cos, sin, seq=seq, tm=tm)
        o_a = _gdn(qa, ka, va, gate, gcb, gct, row(gdn_norm[l]), seq=seq)
        o_d = _diff_attn(qd, kd, vd, row(lambda_q1[l]), row(lambda_k1[l]), row(lambda_q2[l]), row(lambda_k2[l]),
                         row(diff_norm[l]), seq=seq, lam_init=_lambda_init(l), tq=tq)
        wo = w_out[l]
        last = l == depth - 1
        xf = _ffn(xf, row(ffn2_norm[l]), bf(ffn2_w_gate[l]), bf(ffn2_w_up[l]), bf(ffn2_w_down[l]),
                  proj=(o_a, o_d, bf(wo[:GROUP_W]), bf(wo[GROUP_W:])),
                  final_w=row(final_norm) if last else None, tm=tm)
        if not last:
            xf = _ffn(xf, row(ffn1_norm[l + 1]), bf(ffn1_w_gate[l + 1]), bf(ffn1_w_up[l + 1]),
                      bf(ffn1_w_down[l + 1]), tm=tm)
    return xf.reshape(b, seq, d).astype(x.dtype)
```

```python
import functools
import math

import jax
import jax.numpy as jnp
import numpy as np
from jax import lax
from jax.experimental import pallas as pl
from jax.experimental.pallas import tpu as pltpu

F32 = jnp.float32
BF16 = jnp.bfloat16

EPS = 1e-6
ROPE_THETA = 10000.0
HEAD_W = 128
N_HEADS = 4
GROUP_W = N_HEADS * HEAD_W
DIFF_D = 64
LOG2_E = math.log2(math.e)
CONV_K = 4
HALO = 8
GDN_CHUNK = 128
GDN_GROUP = 4
TOKEN_TILE = 512
FFN_COLS = 256
Q_TILE = 256
VMEM_LIMIT = 56 * 1024 * 1024

COL_QKV = 0
COL_GATE = 3 * GROUP_W
COL_QD = COL_GATE + GROUP_W
COL_KD = COL_QD + GROUP_W
COL_VD = COL_KD + GROUP_W
COL_AB = COL_VD + GROUP_W
W_ALL_COLS = COL_AB + HEAD_W


def _dot(a, b):
    return jnp.dot(a, b, preferred_element_type=F32)


def _dot_nt(a, b):
    return lax.dot_general(a, b, (((1,), (1,)), ((), ())), preferred_element_type=F32)


def _rms(x, w):
    return x * lax.rsqrt(jnp.mean(x * x, axis=-1, keepdims=True) + EPS) * w


def _sigmoid(x):
    return 1.0 / (1.0 + jnp.exp(-x))


def _silu(x):
    return x * _sigmoid(x)


def _resident(arr, layer=None):
    if layer is None:
        shape, index = arr.shape, (0,) * arr.ndim
    else:
        shape, index = (None,) + arr.shape[1:], (layer,) + (0,) * (arr.ndim - 1)
    return pl.BlockSpec(shape, lambda *_: index, pipeline_mode=pl.Buffered(1))


def _ffn_body(*refs, pre_proj, post_norm):
    it = iter(refs)
    x_ref = next(it)
    if pre_proj:
        oa_ref, od_ref, wo_ref = next(it), next(it), next(it)
    nw_ref, wg_ref, wu_ref, wd_ref = next(it), next(it), next(it), next(it)
    if post_norm:
        fw_ref = next(it)
    o_ref, a_scr = next(it), next(it)

    x = x_ref[...]
    if pre_proj:
        x = x + _dot(oa_ref[...], wo_ref[0:GROUP_W, :]) + _dot(od_ref[...], wo_ref[GROUP_W:2 * GROUP_W, :])
    h = _rms(x, nw_ref[...]).astype(BF16)
    for c in range(wg_ref.shape[1] // FFN_COLS):
        cols = slice(c * FFN_COLS, (c + 1) * FFN_COLS)
        g = _dot(h, wg_ref[:, cols])
        u = _dot(h, wu_ref[:, cols])
        a_scr[:, cols] = (_silu(g) * u).astype(BF16)
    out = x + 0.5 * _dot(a_scr[...], wd_ref[...])
    if post_norm:
        out = _rms(out, fw_ref[...])
    o_ref[...] = out


def _ffn(x, nw, wg, wu, wd, *, layer, proj=None, final_w=None, tm):
    t, d = x.shape
    d_ff = wg.shape[2]
    row = lambda w: pl.BlockSpec((tm, w), lambda i: (i, 0))
    args, specs = [x], [row(d)]
    if proj is not None:
        oa, od, wo = proj
        args += [oa, od, wo]
        specs += [row(GROUP_W), row(GROUP_W), _resident(wo, layer)]
    args += [nw, wg, wu, wd]
    specs += [_resident(a, layer) for a in (nw, wg, wu, wd)]
    if final_w is not None:
        args.append(final_w)
        specs.append(_resident(final_w))
    return pl.pallas_call(
        functools.partial(_ffn_body, pre_proj=proj is not None, post_norm=final_w is not None),
        grid=(t // tm,),
        in_specs=specs,
        out_specs=row(d),
        out_shape=jax.ShapeDtypeStruct((t, d), F32),
        scratch_shapes=[pltpu.VMEM((tm, d_ff), BF16)],
        compiler_params=pltpu.CompilerParams(dimension_semantics=("parallel",), vmem_limit_bytes=VMEM_LIMIT),
        name="ffn_proj" if proj is not None else "ffn",
    )(*args)


def _split3(x):
    hi = x.astype(BF16)
    r = x - hi.astype(F32)
    mid = r.astype(BF16)
    lo = (r - mid.astype(F32)).astype(BF16)
    return hi, mid, lo


def _in_proj_body(x_ref, nw_ref, w_ref, cw_ref, alog_ref, dtb_ref, cos_ref, sin_ref,
                  qa_ref, ka_ref, va_ref, gate_ref, gcb_ref, gct_ref, qd_ref, kd_ref, vd_ref,
                  pbuf, *, blocks_per_seq):
    tm = x_ref.shape[0]

    @pl.when(pl.program_id(0) % blocks_per_seq == 0)
    def _():
        pbuf[0:HALO, :] = jnp.zeros((HALO, pbuf.shape[1]), F32)

    h = _rms(x_ref[...], nw_ref[...]).astype(BF16)

    def project(col, width=GROUP_W):
        return _dot(h, w_ref[:, col:col + width])

    def conv_section(sec, out_ref):
        cols = slice(sec * GROUP_W, (sec + 1) * GROUP_W)
        p = project(COL_QKV + sec * GROUP_W)
        pbuf[HALO:HALO + tm, cols] = p
        y = p * cw_ref[CONV_K - 1:CONV_K, cols]
        for j in range(CONV_K - 1):
            shift = CONV_K - 1 - j
            y = y + pbuf[HALO - shift:HALO - shift + tm, cols] * cw_ref[j:j + 1, cols]
        pbuf[0:HALO, cols] = pbuf[tm:tm + HALO, cols]
        y = _silu(y)
        if sec < 2:
            scale = HEAD_W ** -0.5 if sec == 0 else 1.0
            for hd in range(N_HEADS):
                hc = slice(hd * HEAD_W, (hd + 1) * HEAD_W)
                yh = y[:, hc]
                yh = yh * (lax.rsqrt(jnp.sum(yh * yh, axis=-1, keepdims=True) + EPS) * scale)
                out_ref[:, hc] = yh.astype(BF16)
        else:
            out_ref[...] = y.astype(BF16)

    def rope_section(col, out_ref, scale):
        cos = cos_ref[...]
        sin = sin_ref[...]
        low_half = (lax.broadcasted_iota(jnp.int32, (tm, HEAD_W), 1) % DIFF_D) < (DIFF_D // 2)
        p = project(col)
        for hd in range(N_HEADS):
            hc = slice(hd * HEAD_W, (hd + 1) * HEAD_W)
            ph = p[:, hc]
            partner = jnp.where(low_half,
                                pltpu.roll(ph, HEAD_W - DIFF_D // 2, 1),
                                pltpu.roll(ph, DIFF_D // 2, 1))
            out_ref[:, hc] = ((ph * cos + partner * sin) * scale).astype(BF16)

    conv_section(0, qa_ref)
    conv_section(1, ka_ref)
    conv_section(2, va_ref)
    gate_ref[...] = project(COL_GATE).astype(BF16)

    ab = project(COL_AB, HEAD_W)
    z = ab + dtb_ref[...]
    softplus = jnp.maximum(z, 0.0) + jnp.log1p(jnp.exp(-jnp.abs(z)))
    lane = lax.broadcasted_iota(jnp.int32, ab.shape, 1)
    g = jnp.where(lane < N_HEADS, -jnp.exp(alog_ref[...]) * softplus, 0.0)
    r = lax.broadcasted_iota(jnp.int32, (GDN_CHUNK, GDN_CHUNK), 0)
    cidx = lax.broadcasted_iota(jnp.int32, (GDN_CHUNK, GDN_CHUNK), 1)
    tri = jnp.where(cidx <= r, 1.0, 0.0).astype(BF16)
    beta = _sigmoid(ab)
    decay_lane = lax.broadcasted_iota(jnp.int32, (GDN_CHUNK, HEAD_W), 1) < N_HEADS
    for cc in range(tm // GDN_CHUNK):
        blk = slice(cc * GDN_CHUNK, (cc + 1) * GDN_CHUNK)
        gc = sum(_dot(tri, part) for part in _split3(g[blk, :]))
        gcb_ref[blk, :] = jnp.where(decay_lane, gc, beta[blk, :])
        gct_ref[:, blk] = gc.T[0:HALO, :]

    rope_section(COL_QD, qd_ref, DIFF_D ** -0.5 * LOG2_E)
    rope_section(COL_KD, kd_ref, 1.0)
    vd_ref[...] = project(COL_VD).astype(BF16)


def _in_proj(x, nw, w_all, cw, alog, dtb, cos, sin, *, layer, seq, tm):
    t, d = x.shape
    blocks_per_seq = seq // tm
    row = lambda w: pl.BlockSpec((tm, w), lambda i: (i, 0))
    tab = pl.BlockSpec((tm, HEAD_W), lambda i: (i % blocks_per_seq, 0))
    res = [nw, w_all, cw, alog, dtb]
    bf = lambda: jax.ShapeDtypeStruct((t, GROUP_W), BF16)
    return pl.pallas_call(
        functools.partial(_in_proj_body, blocks_per_seq=blocks_per_seq),
        grid=(t // tm,),
        in_specs=[row(d)] + [_resident(a, layer) for a in res] + [tab, tab],
        out_specs=[row(GROUP_W)] * 4 + [row(HEAD_W), pl.BlockSpec((HALO, tm), lambda i: (0, i))] + [row(GROUP_W)] * 3,
        out_shape=[bf(), bf(), bf(), bf(), jax.ShapeDtypeStruct((t, HEAD_W), F32),
                   jax.ShapeDtypeStruct((HALO, t), F32), bf(), bf(), bf()],
        scratch_shapes=[pltpu.VMEM((tm + HALO, 3 * GROUP_W), F32)],
        compiler_params=pltpu.CompilerParams(dimension_semantics=("arbitrary",), vmem_limit_bytes=VMEM_LIMIT),
        name="in_proj",
    )(x, *res, cos, sin)


def _block_diag(x):
    half = x.shape[1] // 2
    z = jnp.zeros((x.shape[0], half), x.dtype)
    return jnp.concatenate([jnp.concatenate([x[:, :half], z], axis=1),
                            jnp.concatenate([z, x[:, half:]], axis=1)], axis=0)


def _gdn_body(q_ref, k_ref, v_ref, gate_ref, gcb_ref, gct_ref, nw_ref, o_ref,
              u_scr, w_scr, qd_scr, qk_scr, kdt_scr, erow_scr, state, *, group):
    c = GDN_CHUNK
    pw = 2 * HEAD_W
    n_pairs = N_HEADS // 2
    n_chunks = q_ref.shape[0] // c
    levels = int(math.log2(c))
    row = lax.broadcasted_iota(jnp.int32, (c, pw), 0)
    col = lax.broadcasted_iota(jnp.int32, (c, pw), 1) & (HEAD_W - 1)
    incl = row >= col
    strict = row > col
    eye = jnp.where(row == col, 1.0, 0.0)
    xor = row ^ col
    pair_level = jnp.full((c, pw), -1, jnp.int32)
    for level in range(levels):
        pair_level = jnp.where((xor >> level) == 1, level, pair_level)

    def bcast_pair(arr, lane0, lane1):
        return jnp.concatenate([jnp.broadcast_to(arr[:, lane0:lane0 + 1], (c, HEAD_W)),
                                jnp.broadcast_to(arr[:, lane1:lane1 + 1], (c, HEAD_W))], axis=1)

    def prepare(gi):
        chains = []
        for j in range(group):
            rows = slice((gi * group + j) * c, (gi * group + j + 1) * c)
            gcb = gcb_ref[rows, :]
            g_last = gcb[c - 1:c, :]
            e_gc = jnp.exp(gcb)
            e_rest = jnp.exp(g_last - gcb)
            e_last = jnp.exp(g_last)
            e_rows = slice((gi * group + j) * HALO, (gi * group + j + 1) * HALO)
            erow_scr[e_rows, :] = jnp.concatenate(
                [jnp.broadcast_to(e_last[:, hd:hd + 1], (HALO, HEAD_W)) for hd in range(N_HEADS)], axis=1)
            for p in range(n_pairs):
                h0, h1 = 2 * p, 2 * p + 1
                cols = slice(p * pw, (p + 1) * pw)
                q = q_ref[rows, cols]
                k = k_ref[rows, cols]
                kf = k.astype(F32)
                beta = bcast_pair(gcb, N_HEADS + h0, N_HEADS + h1)
                e_gc_p = bcast_pair(e_gc, h0, h1)
                gc_col = bcast_pair(gcb, h0, h1)
                gc_row = jnp.concatenate([jnp.broadcast_to(gct_ref[h0:h0 + 1, rows], (c, HEAD_W)),
                                          jnp.broadcast_to(gct_ref[h1:h1 + 1, rows], (c, HEAD_W))], axis=1)
                decay = jnp.where(incl, jnp.exp(gc_col - gc_row), 0.0)
                kb = kf * beta
                prod = _dot_nt(jnp.concatenate([kb.astype(BF16), q], axis=0), _block_diag(k))
                m = jnp.where(strict, prod[:c] * decay, 0.0)
                qk_scr[rows, cols] = (prod[c:] * decay).astype(BF16)
                qd_scr[rows, cols] = (q.astype(F32) * e_gc_p).astype(BF16)
                kd = kf * bcast_pair(e_rest, h0, h1)
                kdt_scr[rows, cols] = jnp.concatenate([kd[:, :HEAD_W].T, kd[:, HEAD_W:].T], axis=1).astype(BF16)
                chains.append(dict(
                    rows=rows, cols=cols, neg_m=(-m).astype(BF16),
                    vb=(v_ref[rows, cols].astype(F32) * beta).astype(BF16),
                    kbe=(kb * e_gc_p).astype(BF16),
                    t=eye - jnp.where(pair_level == 0, m, 0.0)))
        for level in range(1, levels):
            for ch in chains:
                ch["t_bf"] = ch["t"].astype(BF16)
                ch["x"] = _dot(ch["neg_m"], _block_diag(ch["t_bf"])).astype(BF16)
            for ch in chains:
                y = _dot(ch["t_bf"], _block_diag(ch["x"]))
                ch["t"] = jnp.where(pair_level == level, y, ch["t"])
        for ch in chains:
            t_bf = ch["t"].astype(BF16)
            u_scr[ch["rows"], ch["cols"]] = _dot(t_bf, _block_diag(ch["vb"]))
            w_scr[ch["rows"], ch["cols"]] = _dot(t_bf, _block_diag(ch["kbe"])).astype(BF16)

    nw = jnp.concatenate([nw_ref[...], nw_ref[...]], axis=1)
    state[...] = jnp.zeros(state.shape, F32)

    def recur(i):
        rows = slice(i * c, (i + 1) * c)
        e_row = erow_scr[i * HALO:i * HALO + 1, :]
        for p in range(n_pairs):
            cols = slice(p * pw, (p + 1) * pw)
            s = state[p]
            r1 = _dot(jnp.concatenate([w_scr[rows, cols], qd_scr[rows, cols]], axis=0),
                      _block_diag(s.astype(BF16)))
            v_new = (u_scr[rows, cols] - r1[:c]).astype(BF16)
            r2 = _dot(jnp.concatenate([kdt_scr[rows, cols], qk_scr[rows, cols]], axis=0),
                      _block_diag(v_new))
            state[p] = s * e_row[:, cols] + r2[:c]
            o = r1[c:] + r2[c:]
            ms = jnp.concatenate(
                [jnp.broadcast_to(jnp.mean(o[:, hh * HEAD_W:(hh + 1) * HEAD_W] ** 2, axis=-1, keepdims=True),
                                  (c, HEAD_W)) for hh in range(2)], axis=1)
            o = o * lax.rsqrt(ms + EPS) * nw * _silu(gate_ref[rows, cols].astype(F32))
            o_ref[rows, cols] = o.astype(BF16)

    n_groups = n_chunks // group
    for gi in range(n_groups + 1):
        if gi < n_groups:
            prepare(gi)
        if gi > 0:
            for i in range((gi - 1) * group, gi * group):
                recur(i)


def _gdn(q, k, v, gate, gcb, gct, nw, *, layer, seq):
    t = q.shape[0]
    n_chunks = seq // GDN_CHUNK
    seq_blk = lambda w: pl.BlockSpec((seq, w), lambda b: (b, 0))
    return pl.pallas_call(
        functools.partial(_gdn_body, group=math.gcd(GDN_GROUP, n_chunks)),
        grid=(t // seq,),
        in_specs=[seq_blk(GROUP_W)] * 4 + [seq_blk(HEAD_W), pl.BlockSpec((HALO, seq), lambda b: (0, b)),
                                           _resident(nw, layer)],
        out_specs=seq_blk(GROUP_W),
        out_shape=jax.ShapeDtypeStruct((t, GROUP_W), BF16),
        scratch_shapes=[pltpu.VMEM((seq, GROUP_W), F32)] + [pltpu.VMEM((seq, GROUP_W), BF16)] * 4
                       + [pltpu.VMEM((n_chunks * HALO, GROUP_W), F32),
                          pltpu.VMEM((N_HEADS // 2, GDN_CHUNK, 2 * HEAD_W), F32)],
        compiler_params=pltpu.CompilerParams(dimension_semantics=("parallel",), vmem_limit_bytes=VMEM_LIMIT),
        name="gdn",
    )(q, k, v, gate, gcb, gct, nw)


def _block_order(n):
    long_first = list(range(n - 1, -1, -1))
    head, rest = long_first[:3], long_first[3:]
    order = list(head)
    while rest:
        order.append(rest.pop())
        if rest:
            order.append(rest.pop(0))
    return order


def _diff_body(q_ref, k_ref, v_ref, lq1_ref, lk1_ref, lq2_ref, lk2_ref, nw_ref, o_ref,
               v1_scr, *, lam_init, tq):
    seq = q_ref.shape[0]
    rows2 = 2 * tq
    lane = lax.broadcasted_iota(jnp.int32, (tq, HEAD_W), 1)
    v1_scr[:, :HEAD_W] = v_ref[...]
    v1_scr[:, HEAD_W:] = jnp.where(lax.broadcasted_iota(jnp.int32, (seq, HEAD_W), 1) == 0, 1.0, 0.0).astype(BF16)
    r = lax.broadcasted_iota(jnp.int32, (rows2, tq), 0) & (tq - 1)
    keep = lax.broadcasted_iota(jnp.int32, (rows2, tq), 1) <= r
    lam = (jnp.exp(jnp.sum(lq1_ref[...] * lk1_ref[...], axis=-1, keepdims=True))
           - jnp.exp(jnp.sum(lq2_ref[...] * lk2_ref[...], axis=-1, keepdims=True)) + lam_init)
    nw = nw_ref[...]

    for i in _block_order(seq // tq):
        q_rows = slice(i * tq, (i + 1) * tq)
        n_keys = (i + 1) * tq
        q = q_ref[q_rows, :]
        zero = jnp.zeros_like(q)
        q2 = jnp.concatenate([jnp.where(lane < DIFF_D, q, zero), jnp.where(lane >= DIFF_D, q, zero)], axis=0)
        s = _dot_nt(q2, k_ref[0:n_keys, :])
        s_diag = jnp.where(keep, s[:, i * tq:], -jnp.inf)
        s = s_diag if i == 0 else jnp.concatenate([s[:, :i * tq], s_diag], axis=1)
        m = jnp.max(s, axis=-1, keepdims=True)
        p = jnp.exp2((s - m).astype(BF16))
        acc = _dot(p, v1_scr[0:n_keys, :])
        o = acc[:, :HEAD_W] / acc[:, HEAD_W:HEAD_W + 1]
        o = o[:tq] - lam * o[tq:]
        o_ref[q_rows, :] = (_rms(o, nw) * (1.0 - lam_init)).astype(BF16)


def _diff_attn(q, k, v, lq1, lk1, lq2, lk2, nw, *, layer, seq, lam_init, tq):
    t = q.shape[0]
    blk = pl.BlockSpec((seq, HEAD_W), lambda b, h: (b, h))
    small = [lq1, lk1, lq2, lk2, nw]
    return pl.pallas_call(
        functools.partial(_diff_body, lam_init=lam_init, tq=tq),
        grid=(t // seq, N_HEADS),
        in_specs=[blk, blk, blk] + [_resident(a, layer) for a in small],
        out_specs=blk,
        out_shape=jax.ShapeDtypeStruct((t, GROUP_W), BF16),
        scratch_shapes=[pltpu.VMEM((seq, 2 * HEAD_W), BF16)],
        compiler_params=pltpu.CompilerParams(
            dimension_semantics=("parallel", "parallel"), vmem_limit_bytes=VMEM_LIMIT),
        name="diff_attn",
    )(q, k, v, *small)


def _rope_tables(seq):
    inv = 1.0 / (ROPE_THETA ** (np.arange(0, DIFF_D, 2, dtype=np.float32) / DIFF_D))
    ang = np.arange(seq, dtype=np.float32)[:, None] * inv[None, :].astype(np.float32)
    cos, sin = np.cos(ang).astype(np.float32), np.sin(ang).astype(np.float32)
    reps = HEAD_W // DIFF_D
    return (jnp.asarray(np.tile(np.concatenate([cos, cos], axis=-1), (1, reps))),
            jnp.asarray(np.tile(np.concatenate([-sin, sin], axis=-1), (1, reps))))


def _lambda_init(layer):
    return 0.8 - 0.6 * math.exp(-0.3 * layer)


def kernel(x, ffn1_norm, ffn1_w_gate, ffn1_w_up, ffn1_w_down, mix_norm, w_in, conv_w, a_log, dt_bias, gdn_norm, lambda_q1, lambda_k1, lambda_q2, lambda_k2, diff_norm, w_out, ffn2_norm, ffn2_w_gate, ffn2_w_up, ffn2_w_down, final_norm):
    b, seq, d = x.shape
    depth = w_in.shape[0]
    t = b * seq
    tm = min(TOKEN_TILE, seq)
    tq = min(Q_TILE, seq)
    assert seq % GDN_CHUNK == 0 and seq % tm == 0 and seq % tq == 0
    n_qkv_gate = 4 * GROUP_W
    assert w_in.shape[2] == n_qkv_gate + 2 * N_HEADS + 3 * GROUP_W
    cos, sin = _rope_tables(seq)
    bf = lambda a: a.astype(BF16)
    rows = lambda a: a.reshape(depth, 1, -1).astype(F32)
    pad_lanes = lambda a: jnp.pad(rows(a), ((0, 0), (0, 0), (0, HEAD_W - a.shape[1])))

    w_all = bf(jnp.concatenate(
        [w_in[:, :, :n_qkv_gate], w_in[:, :, n_qkv_gate + 2 * N_HEADS:],
         w_in[:, :, n_qkv_gate:n_qkv_gate + 2 * N_HEADS],
         jnp.zeros((depth, d, HEAD_W - 2 * N_HEADS), w_in.dtype)], axis=2))
    ffn1 = (rows(ffn1_norm), bf(ffn1_w_gate), bf(ffn1_w_up), bf(ffn1_w_down))
    ffn2 = (rows(ffn2_norm), bf(ffn2_w_gate), bf(ffn2_w_up), bf(ffn2_w_down))
    wo = bf(w_out)
    mix_nw, gdn_nw, diff_nw = rows(mix_norm), rows(gdn_norm), rows(diff_norm)
    lams = [rows(a) for a in (lambda_q1, lambda_k1, lambda_q2, lambda_k2)]
    alog, dtb = pad_lanes(a_log), pad_lanes(dt_bias)
    cw = conv_w.astype(F32)

    xf = x.reshape(t, d).astype(F32)
    xf = _ffn(xf, *ffn1, layer=0, tm=tm)
    for l in range(depth):
        qa, ka, va, gate, gcb, gct, qd, kd, vd = _in_proj(
            xf, mix_nw, w_all, cw, alog, dtb, cos, sin, layer=l, seq=seq, tm=tm)
        o_a = _gdn(qa, ka, va, gate, gcb, gct, gdn_nw, layer=l, seq=seq)
        o_d = _diff_attn(qd, kd, vd, *lams, diff_nw, layer=l, seq=seq, lam_init=_lambda_init(l), tq=tq)
        last = l == depth - 1
        xf = _ffn(xf, *ffn2, layer=l, proj=(o_a, o_d, wo),
                  final_w=final_norm.reshape(1, d).astype(F32) if last else None, tm=tm)
        if not last:
            xf = _ffn(xf, *ffn1, layer=l + 1, tm=tm)
    return xf.reshape(b, seq, d).astype(x.dtype)
```

```python
import functools
import math

import jax
import jax.numpy as jnp
import numpy as np
from jax import lax
from jax.experimental import pallas as pl
from jax.experimental.pallas import tpu as pltpu

F32 = jnp.float32
BF16 = jnp.bfloat16

EPS = 1e-6
ROPE_THETA = 10000.0
HEAD_W = 128
N_HEADS = 4
GROUP_W = N_HEADS * HEAD_W
DIFF_D = 64
LOG2_E = math.log2(math.e)
CONV_K = 4
HALO = 8
GDN_CHUNK = 128
GDN_GROUP = 4
TOKEN_TILE = 512
FFN_COLS = 256
Q_TILE = 256
ATTN_HEADS_PER_STEP = 2
VMEM_LIMIT = 56 * 1024 * 1024

COL_QKV = 0
COL_GATE = 3 * GROUP_W
COL_QD = COL_GATE + GROUP_W
COL_KD = COL_QD + GROUP_W
COL_VD = COL_KD + GROUP_W
COL_AB = COL_VD + GROUP_W
W_ALL_COLS = COL_AB + HEAD_W


def _dot(a, b):
    return jnp.dot(a, b, preferred_element_type=F32)


def _dot_nt(a, b):
    return lax.dot_general(a, b, (((1,), (1,)), ((), ())), preferred_element_type=F32)


def _rms(x, w):
    return x * lax.rsqrt(jnp.mean(x * x, axis=-1, keepdims=True) + EPS) * w


def _sigmoid(x):
    return 1.0 / (1.0 + jnp.exp(-x))


def _silu(x):
    return x * _sigmoid(x)


def _resident(arr, layer=None):
    if layer is None:
        shape, index = arr.shape, (0,) * arr.ndim
    else:
        shape, index = (None,) + arr.shape[1:], (layer,) + (0,) * (arr.ndim - 1)
    return pl.BlockSpec(shape, lambda *_: index, pipeline_mode=pl.Buffered(1))


def _ffn_body(*refs, pre_proj, post_norm):
    it = iter(refs)
    x_ref = next(it)
    if pre_proj:
        oa_ref, od_ref, wo_ref = next(it), next(it), next(it)
    nw_ref, wg_ref, wu_ref, wd_ref = next(it), next(it), next(it), next(it)
    if post_norm:
        fw_ref = next(it)
    o_ref, a_scr = next(it), next(it)

    x = x_ref[...]
    if pre_proj:
        x = x + _dot(oa_ref[...], wo_ref[0:GROUP_W, :]) + _dot(od_ref[...], wo_ref[GROUP_W:2 * GROUP_W, :])
    h = _rms(x, nw_ref[...]).astype(BF16)
    for c in range(wg_ref.shape[1] // FFN_COLS):
        cols = slice(c * FFN_COLS, (c + 1) * FFN_COLS)
        g = _dot(h, wg_ref[:, cols])
        u = _dot(h, wu_ref[:, cols])
        a_scr[:, cols] = (_silu(g) * u).astype(BF16)
    out = x + 0.5 * _dot(a_scr[...], wd_ref[...])
    if post_norm:
        out = _rms(out, fw_ref[...])
    o_ref[...] = out


def _ffn(x, nw, wg, wu, wd, *, layer, proj=None, final_w=None, tm):
    t, d = x.shape
    d_ff = wg.shape[2]
    row = lambda w: pl.BlockSpec((tm, w), lambda i: (i, 0))
    args, specs = [x], [row(d)]
    if proj is not None:
        oa, od, wo = proj
        args += [oa, od, wo]
        specs += [row(GROUP_W), row(GROUP_W), _resident(wo, layer)]
    args += [nw, wg, wu, wd]
    specs += [_resident(a, layer) for a in (nw, wg, wu, wd)]
    if final_w is not None:
        args.append(final_w)
        specs.append(_resident(final_w))
    return pl.pallas_call(
        functools.partial(_ffn_body, pre_proj=proj is not None, post_norm=final_w is not None),
        grid=(t // tm,),
        in_specs=specs,
        out_specs=row(d),
        out_shape=jax.ShapeDtypeStruct((t, d), F32),
        scratch_shapes=[pltpu.VMEM((tm, d_ff), BF16)],
        compiler_params=pltpu.CompilerParams(dimension_semantics=("parallel",), vmem_limit_bytes=VMEM_LIMIT),
        name="ffn_proj" if proj is not None else "ffn",
    )(*args)


def _split3(x):
    hi = x.astype(BF16)
    r = x - hi.astype(F32)
    mid = r.astype(BF16)
    lo = (r - mid.astype(F32)).astype(BF16)
    return hi, mid, lo


def _in_proj_body(x_ref, nw_ref, w_ref, cw_ref, alog_ref, dtb_ref, cos_ref, sin_ref,
                  qa_ref, ka_ref, va_ref, gate_ref, gcb_ref, gct_ref, qd_ref, kd_ref, vd_ref,
                  pbuf, *, blocks_per_seq):
    tm = x_ref.shape[0]

    @pl.when(pl.program_id(0) % blocks_per_seq == 0)
    def _():
        pbuf[0:HALO, :] = jnp.zeros((HALO, pbuf.shape[1]), F32)

    h = _rms(x_ref[...], nw_ref[...]).astype(BF16)

    def project(col, width=GROUP_W):
        return _dot(h, w_ref[:, col:col + width])

    def conv_section(sec, out_ref):
        cols = slice(sec * GROUP_W, (sec + 1) * GROUP_W)
        p = project(COL_QKV + sec * GROUP_W)
        pbuf[HALO:HALO + tm, cols] = p
        y = p * cw_ref[CONV_K - 1:CONV_K, cols]
        for j in range(CONV_K - 1):
            shift = CONV_K - 1 - j
            y = y + pbuf[HALO - shift:HALO - shift + tm, cols] * cw_ref[j:j + 1, cols]
        pbuf[0:HALO, cols] = pbuf[tm:tm + HALO, cols]
        y = _silu(y)
        if sec < 2:
            scale = HEAD_W ** -0.5 if sec == 0 else 1.0
            for hd in range(N_HEADS):
                hc = slice(hd * HEAD_W, (hd + 1) * HEAD_W)
                yh = y[:, hc]
                yh = yh * (lax.rsqrt(jnp.sum(yh * yh, axis=-1, keepdims=True) + EPS) * scale)
                out_ref[:, hc] = yh.astype(BF16)
        else:
            out_ref[...] = y.astype(BF16)

    def rope_section(col, out_ref, scale):
        cos = cos_ref[...]
        sin = sin_ref[...]
        low_half = (lax.broadcasted_iota(jnp.int32, (tm, HEAD_W), 1) % DIFF_D) < (DIFF_D // 2)
        p = project(col)
        for hd in range(N_HEADS):
            hc = slice(hd * HEAD_W, (hd + 1) * HEAD_W)
            ph = p[:, hc]
            partner = jnp.where(low_half,
                                pltpu.roll(ph, HEAD_W - DIFF_D // 2, 1),
                                pltpu.roll(ph, DIFF_D // 2, 1))
            out_ref[:, hc] = ((ph * cos + partner * sin) * scale).astype(BF16)

    conv_section(0, qa_ref)
    conv_section(1, ka_ref)
    conv_section(2, va_ref)
    gate_ref[...] = project(COL_GATE).astype(BF16)

    ab = project(COL_AB, HEAD_W)
    z = ab + dtb_ref[...]
    softplus = jnp.maximum(z, 0.0) + jnp.log1p(jnp.exp(-jnp.abs(z)))
    lane = lax.broadcasted_iota(jnp.int32, ab.shape, 1)
    g = jnp.where(lane < N_HEADS, -jnp.exp(alog_ref[...]) * softplus, 0.0)
    r = lax.broadcasted_iota(jnp.int32, (GDN_CHUNK, GDN_CHUNK), 0)
    cidx = lax.broadcasted_iota(jnp.int32, (GDN_CHUNK, GDN_CHUNK), 1)
    tri = jnp.where(cidx <= r, 1.0, 0.0).astype(BF16)
    beta = _sigmoid(ab)
    decay_lane = lax.broadcasted_iota(jnp.int32, (GDN_CHUNK, HEAD_W), 1) < N_HEADS
    for cc in range(tm // GDN_CHUNK):
        blk = slice(cc * GDN_CHUNK, (cc + 1) * GDN_CHUNK)
        gc = sum(_dot(tri, part) for part in _split3(g[blk, :]))
        gcb_ref[blk, :] = jnp.where(decay_lane, gc, beta[blk, :])
        gct_ref[:, blk] = gc.T[0:HALO, :]

    rope_section(COL_QD, qd_ref, DIFF_D ** -0.5 * LOG2_E)
    rope_section(COL_KD, kd_ref, 1.0)
    vd_ref[...] = project(COL_VD).astype(BF16)


def _in_proj(x, nw, w_all, cw, alog, dtb, cos, sin, *, layer, seq, tm):
    t, d = x.shape
    blocks_per_seq = seq // tm
    row = lambda w: pl.BlockSpec((tm, w), lambda i: (i, 0))
    tab = pl.BlockSpec((tm, HEAD_W), lambda i: (i % blocks_per_seq, 0))
    res = [nw, w_all, cw, alog, dtb]
    bf = lambda: jax.ShapeDtypeStruct((t, GROUP_W), BF16)
    return pl.pallas_call(
        functools.partial(_in_proj_body, blocks_per_seq=blocks_per_seq),
        grid=(t // tm,),
        in_specs=[row(d)] + [_resident(a, layer) for a in res] + [tab, tab],
        out_specs=[row(GROUP_W)] * 4 + [row(HEAD_W), pl.BlockSpec((HALO, tm), lambda i: (0, i))] + [row(GROUP_W)] * 3,
        out_shape=[bf(), bf(), bf(), bf(), jax.ShapeDtypeStruct((t, HEAD_W), F32),
                   jax.ShapeDtypeStruct((HALO, t), F32), bf(), bf(), bf()],
        scratch_shapes=[pltpu.VMEM((tm + HALO, 3 * GROUP_W), F32)],
        compiler_params=pltpu.CompilerParams(dimension_semantics=("arbitrary",), vmem_limit_bytes=VMEM_LIMIT),
        name="in_proj",
    )(x, *res, cos, sin)


def _block_diag(x):
    half = x.shape[1] // 2
    z = jnp.zeros((x.shape[0], half), x.dtype)
    return jnp.concatenate([jnp.concatenate([x[:, :half], z], axis=1),
                            jnp.concatenate([z, x[:, half:]], axis=1)], axis=0)


def _gdn_body(q_ref, k_ref, v_ref, gate_ref, gcb_ref, gct_ref, nw_ref, o_ref,
              u_scr, w_scr, qd_scr, qk_scr, kdt_scr, erow_scr, state, *, group):
    c = GDN_CHUNK
    pw = 2 * HEAD_W
    n_pairs = N_HEADS // 2
    n_chunks = q_ref.shape[0] // c
    levels = int(math.log2(c))
    row = lax.broadcasted_iota(jnp.int32, (c, pw), 0)
    col = lax.broadcasted_iota(jnp.int32, (c, pw), 1) & (HEAD_W - 1)
    incl = row >= col
    strict = row > col
    eye = jnp.where(row == col, 1.0, 0.0)
    xor = row ^ col
    pair_level = jnp.full((c, pw), -1, jnp.int32)
    for level in range(levels):
        pair_level = jnp.where((xor >> level) == 1, level, pair_level)

    def bcast_pair(arr, lane0, lane1):
        return jnp.concatenate([jnp.broadcast_to(arr[:, lane0:lane0 + 1], (c, HEAD_W)),
                                jnp.broadcast_to(arr[:, lane1:lane1 + 1], (c, HEAD_W))], axis=1)

    def odd_rows(a, level):
        b = 1 << level
        return jnp.concatenate([a[r * b:(r + 1) * b] for r in range(1, c // b, 2)], axis=0)

    def spread_odd(a_odd, other, level):
        b = 1 << level
        return jnp.concatenate([a_odd[(r // 2) * b:(r // 2 + 1) * b] if r % 2 else other[r * b:(r + 1) * b]
                                for r in range(c // b)], axis=0)

    zeros_pair = jnp.zeros((c, pw), F32)
    vreg_level = 3

    def setup(ci):
        rows = slice(ci * c, (ci + 1) * c)
        gcb = gcb_ref[rows, :]
        g_last = gcb[c - 1:c, :]
        e_gc = jnp.exp(gcb)
        e_rest = jnp.exp(g_last - gcb)
        e_last = jnp.exp(g_last)
        erow_scr[ci * HALO:(ci + 1) * HALO, :] = jnp.concatenate(
            [jnp.broadcast_to(e_last[:, hd:hd + 1], (HALO, HEAD_W)) for hd in range(N_HEADS)], axis=1)
        chains = []
        for p in range(n_pairs):
            h0, h1 = 2 * p, 2 * p + 1
            cols = slice(p * pw, (p + 1) * pw)
            q = q_ref[rows, cols]
            k = k_ref[rows, cols]
            kf = k.astype(F32)
            beta = bcast_pair(gcb, N_HEADS + h0, N_HEADS + h1)
            e_gc_p = bcast_pair(e_gc, h0, h1)
            gc_col = bcast_pair(gcb, h0, h1)
            gc_row = jnp.concatenate([jnp.broadcast_to(gct_ref[h0:h0 + 1, rows], (c, HEAD_W)),
                                      jnp.broadcast_to(gct_ref[h1:h1 + 1, rows], (c, HEAD_W))], axis=1)
            decay = jnp.where(incl, jnp.exp(gc_col - gc_row), 0.0)
            kb = kf * beta
            prod = _dot_nt(jnp.concatenate([kb.astype(BF16), q], axis=0), _block_diag(k))
            m = jnp.where(strict, prod[:c] * decay, 0.0)
            qk_scr[rows, cols] = (prod[c:] * decay).astype(BF16)
            qd_scr[rows, cols] = (q.astype(F32) * e_gc_p).astype(BF16)
            kd = kf * bcast_pair(e_rest, h0, h1)
            kdt_scr[rows, cols] = jnp.concatenate([kd[:, :HEAD_W].T, kd[:, HEAD_W:].T], axis=1).astype(BF16)
            neg_m = -m
            chains.append(dict(
                rows=rows, cols=cols, neg_m=neg_m.astype(BF16),
                neg_m_odd={lv: odd_rows(neg_m, lv).astype(BF16) for lv in range(vreg_level, levels)},
                vb=(v_ref[rows, cols].astype(F32) * beta).astype(BF16),
                kbe=(kb * e_gc_p).astype(BF16),
                t=eye - jnp.where(pair_level == 0, m, 0.0)))
        return chains

    def level_products(chains, level):
        for ch in chains:
            ch["t_bf"] = ch["t"].astype(BF16)
            if level < vreg_level:
                ch["x"] = _dot(ch["neg_m"], _block_diag(ch["t_bf"])).astype(BF16)
            else:
                x_odd = _dot(ch["neg_m_odd"][level], _block_diag(ch["t_bf"]))
                ch["x"] = spread_odd(x_odd, zeros_pair, level).astype(BF16)

    def level_update(chains, level):
        for ch in chains:
            if level < vreg_level:
                y = _dot(ch["t_bf"], _block_diag(ch["x"]))
                ch["t"] = jnp.where(pair_level == level, y, ch["t"])
            else:
                t_odd = odd_rows(ch["t"], level)
                y_odd = _dot(t_odd.astype(BF16), _block_diag(ch["x"]))
                t_odd = jnp.where(odd_rows(pair_level, level) == level, y_odd, t_odd)
                ch["t"] = spread_odd(t_odd, ch["t"], level)

    def solve(chains):
        for ch in chains:
            t_bf = ch["t"].astype(BF16)
            u_scr[ch["rows"], ch["cols"]] = _dot(t_bf, _block_diag(ch["vb"]))
            w_scr[ch["rows"], ch["cols"]] = _dot(t_bf, _block_diag(ch["kbe"])).astype(BF16)

    nw = jnp.concatenate([nw_ref[...], nw_ref[...]], axis=1)
    state[...] = jnp.zeros(state.shape, F32)

    def recur(i):
        rows = slice(i * c, (i + 1) * c)
        e_row = erow_scr[i * HALO:i * HALO + 1, :]
        for p in range(n_pairs):
            cols = slice(p * pw, (p + 1) * pw)
            s = state[p]
            r1 = _dot(jnp.concatenate([w_scr[rows, cols], qd_scr[rows, cols]], axis=0),
                      _block_diag(s.astype(BF16)))
            v_new = (u_scr[rows, cols] - r1[:c]).astype(BF16)
            r2 = _dot(jnp.concatenate([kdt_scr[rows, cols], qk_scr[rows, cols]], axis=0),
                      _block_diag(v_new))
            state[p] = s * e_row[:, cols] + r2[:c]
            o = r1[c:] + r2[c:]
            ms = jnp.concatenate(
                [jnp.broadcast_to(jnp.mean(o[:, hh * HEAD_W:(hh + 1) * HEAD_W] ** 2, axis=-1, keepdims=True),
                                  (c, HEAD_W)) for hh in range(2)], axis=1)
            o = o * lax.rsqrt(ms + EPS) * nw * _silu(gate_ref[rows, cols].astype(F32))
            o_ref[rows, cols] = o.astype(BF16)

    n_groups = n_chunks // group
    prepared = {}
    for it in range(n_groups + 2):
        mxu_steps = []
        if 0 <= it - 1 < n_groups:
            chains = prepared.pop(it - 1)
            for level in range(1, levels):
                mxu_steps.append(functools.partial(level_products, chains, level))
                mxu_steps.append(functools.partial(level_update, chains, level))
            mxu_steps.append(functools.partial(solve, chains))
        fill_steps = []
        if it < n_groups:
            prepared[it] = []
            for ci in range(it * group, (it + 1) * group):
                fill_steps.append(lambda ci=ci, it=it: prepared[it].extend(setup(ci)))
        if 0 <= it - 2 < n_groups:
            rec = [functools.partial(recur, i) for i in range((it - 2) * group, (it - 1) * group)]
            fill_steps = [s for pair in zip(fill_steps, rec) for s in pair] if fill_steps else rec
        stride = max(1, len(mxu_steps) // max(1, len(fill_steps)))
        for n, step in enumerate(mxu_steps):
            step()
            if n % stride == stride - 1 and fill_steps:
                fill_steps.pop(0)()
        for step in fill_steps:
            step()


def _gdn(q, k, v, gate, gcb, gct, nw, *, layer, seq):
    t = q.shape[0]
    n_chunks = seq // GDN_CHUNK
    seq_blk = lambda w: pl.BlockSpec((seq, w), lambda b: (b, 0))
    return pl.pallas_call(
        functools.partial(_gdn_body, group=math.gcd(GDN_GROUP, n_chunks)),
        grid=(t // seq,),
        in_specs=[seq_blk(GROUP_W)] * 4 + [seq_blk(HEAD_W), pl.BlockSpec((HALO, seq), lambda b: (0, b)),
                                           _resident(nw, layer)],
        out_specs=seq_blk(GROUP_W),
        out_shape=jax.ShapeDtypeStruct((t, GROUP_W), BF16),
        scratch_shapes=[pltpu.VMEM((seq, GROUP_W), F32)] + [pltpu.VMEM((seq, GROUP_W), BF16)] * 4
                       + [pltpu.VMEM((n_chunks * HALO, GROUP_W), F32),
                          pltpu.VMEM((N_HEADS // 2, GDN_CHUNK, 2 * HEAD_W), F32)],
        compiler_params=pltpu.CompilerParams(dimension_semantics=("parallel",), vmem_limit_bytes=VMEM_LIMIT),
        name="gdn",
    )(q, k, v, gate, gcb, gct, nw)


def _block_order(n):
    long_first = list(range(n - 1, -1, -1))
    head, rest = long_first[:3], long_first[3:]
    order = list(head)
    while rest:
        order.append(rest.pop())
        if rest:
            order.append(rest.pop(0))
    return order


def _diff_body(q_ref, k_ref, v_ref, lq1_ref, lk1_ref, lq2_ref, lk2_ref, nw_ref, o_ref,
               v1_scr, *, lam_init, tq, heads):
    seq = q_ref.shape[0]
    rows2 = 2 * tq
    lane = lax.broadcasted_iota(jnp.int32, (tq, HEAD_W), 1)
    unit = jnp.where(lax.broadcasted_iota(jnp.int32, (seq, HEAD_W), 1) == 0, 1.0, 0.0).astype(BF16)
    for hd in range(heads):
        v1_scr[:, 2 * hd * HEAD_W:(2 * hd + 1) * HEAD_W] = v_ref[:, hd * HEAD_W:(hd + 1) * HEAD_W]
        v1_scr[:, (2 * hd + 1) * HEAD_W:(2 * hd + 2) * HEAD_W] = unit
    r = lax.broadcasted_iota(jnp.int32, (rows2, tq), 0) & (tq - 1)
    keep = lax.broadcasted_iota(jnp.int32, (rows2, tq), 1) <= r
    lam = (jnp.exp(jnp.sum(lq1_ref[...] * lk1_ref[...], axis=-1, keepdims=True))
           - jnp.exp(jnp.sum(lq2_ref[...] * lk2_ref[...], axis=-1, keepdims=True)) + lam_init)
    nw = nw_ref[...]

    for i in _block_order(seq // tq):
        for hd in range(heads):
            hc = slice(hd * HEAD_W, (hd + 1) * HEAD_W)
            q_rows = slice(i * tq, (i + 1) * tq)
            n_keys = (i + 1) * tq
            q = q_ref[q_rows, hc]
            zero = jnp.zeros_like(q)
            q2 = jnp.concatenate([jnp.where(lane < DIFF_D, q, zero), jnp.where(lane >= DIFF_D, q, zero)], axis=0)
            s = _dot_nt(q2, k_ref[0:n_keys, hc])
            s_diag = jnp.where(keep, s[:, i * tq:], -jnp.inf)
            s = s_diag if i == 0 else jnp.concatenate([s[:, :i * tq], s_diag], axis=1)
            m = jnp.max(s, axis=-1, keepdims=True)
            p = jnp.exp2((s - m).astype(BF16))
            acc = _dot(p, v1_scr[0:n_keys, 2 * hd * HEAD_W:(2 * hd + 2) * HEAD_W])
            o = acc[:, :HEAD_W] / acc[:, HEAD_W:HEAD_W + 1]
            o = o[:tq] - lam * o[tq:]
            o_ref[q_rows, hc] = (_rms(o, nw) * (1.0 - lam_init)).astype(BF16)


def _diff_attn(q, k, v, lq1, lk1, lq2, lk2, nw, *, layer, seq, lam_init, tq):
    t = q.shape[0]
    heads = ATTN_HEADS_PER_STEP
    blk = pl.BlockSpec((seq, heads * HEAD_W), lambda b, h: (b, h))
    small = [lq1, lk1, lq2, lk2, nw]
    return pl.pallas_call(
        functools.partial(_diff_body, lam_init=lam_init, tq=tq, heads=heads),
        grid=(t // seq, N_HEADS // heads),
        in_specs=[blk, blk, blk] + [_resident(a, layer) for a in small],
        out_specs=blk,
        out_shape=jax.ShapeDtypeStruct((t, GROUP_W), BF16),
        scratch_shapes=[pltpu.VMEM((seq, 2 * heads * HEAD_W), BF16)],
        compiler_params=pltpu.CompilerParams(
            dimension_semantics=("parallel", "parallel"), vmem_limit_bytes=VMEM_LIMIT),
        name="diff_attn",
    )(q, k, v, *small)


def _rope_tables(seq):
    inv = 1.0 / (ROPE_THETA ** (np.arange(0, DIFF_D, 2, dtype=np.float32) / DIFF_D))
    ang = np.arange(seq, dtype=np.float32)[:, None] * inv[None, :].astype(np.float32)
    cos, sin = np.cos(ang).astype(np.float32), np.sin(ang).astype(np.float32)
    reps = HEAD_W // DIFF_D
    return (jnp.asarray(np.tile(np.concatenate([cos, cos], axis=-1), (1, reps))),
            jnp.asarray(np.tile(np.concatenate([-sin, sin], axis=-1), (1, reps))))


def _lambda_init(layer):
    return 0.8 - 0.6 * math.exp(-0.3 * layer)


def kernel(x, ffn1_norm, ffn1_w_gate, ffn1_w_up, ffn1_w_down, mix_norm, w_in, conv_w, a_log, dt_bias, gdn_norm, lambda_q1, lambda_k1, lambda_q2, lambda_k2, diff_norm, w_out, ffn2_norm, ffn2_w_gate, ffn2_w_up, ffn2_w_down, final_norm):
    b, seq, d = x.shape
    depth = w_in.shape[0]
    t = b * seq
    tm = min(TOKEN_TILE, seq)
    tq = min(Q_TILE, seq)
    assert seq % GDN_CHUNK == 0 and seq % tm == 0 and seq % tq == 0
    n_qkv_gate = 4 * GROUP_W
    assert w_in.shape[2] == n_qkv_gate + 2 * N_HEADS + 3 * GROUP_W
    cos, sin = _rope_tables(seq)
    bf = lambda a: a.astype(BF16)
    rows = lambda a: a.reshape(depth, 1, -1).astype(F32)
    pad_lanes = lambda a: jnp.pad(rows(a), ((0, 0), (0, 0), (0, HEAD_W - a.shape[1])))

    w_all = bf(jnp.concatenate(
        [w_in[:, :, :n_qkv_gate], w_in[:, :, n_qkv_gate + 2 * N_HEADS:],
         w_in[:, :, n_qkv_gate:n_qkv_gate + 2 * N_HEADS],
         jnp.zeros((depth, d, HEAD_W - 2 * N_HEADS), w_in.dtype)], axis=2))
    ffn1 = (rows(ffn1_norm), bf(ffn1_w_gate), bf(ffn1_w_up), bf(ffn1_w_down))
    ffn2 = (rows(ffn2_norm), bf(ffn2_w_gate), bf(ffn2_w_up), bf(ffn2_w_down))
    wo = bf(w_out)
    mix_nw, gdn_nw, diff_nw = rows(mix_norm), rows(gdn_norm), rows(diff_norm)
    lams = [rows(a) for a in (lambda_q1, lambda_k1, lambda_q2, lambda_k2)]
    alog, dtb = pad_lanes(a_log), pad_lanes(dt_bias)
    cw = conv_w.astype(F32)

    xf = x.reshape(t, d).astype(F32)
    xf = _ffn(xf, *ffn1, layer=0, tm=tm)
    for l in range(depth):
        qa, ka, va, gate, gcb, gct, qd, kd, vd = _in_proj(
            xf, mix_nw, w_all, cw, alog, dtb, cos, sin, layer=l, seq=seq, tm=tm)
        o_a = _gdn(qa, ka, va, gate, gcb, gct, gdn_nw, layer=l, seq=seq)
        o_d = _diff_attn(qd, kd, vd, *lams, diff_nw, layer=l, seq=seq, lam_init=_lambda_init(l), tq=tq)
        last = l == depth - 1
        xf = _ffn(xf, *ffn2, layer=l, proj=(o_a, o_d, wo),
                  final_w=final_norm.reshape(1, d).astype(F32) if last else None, tm=tm)
        if not last:
            xf = _ffn(xf, *ffn1, layer=l + 1, tm=tm)
    return xf.reshape(b, seq, d).astype(x.dtype)
```

```python
import functools
import math

import jax
import jax.numpy as jnp
import numpy as np
from jax import lax
from jax.experimental import pallas as pl
from jax.experimental.pallas import tpu as pltpu

F32 = jnp.float32
BF16 = jnp.bfloat16

EPS = 1e-6
ROPE_THETA = 10000.0
HEAD_W = 128
N_HEADS = 4
GROUP_W = N_HEADS * HEAD_W
DIFF_D = 64
LOG2_E = math.log2(math.e)
CONV_K = 4
HALO = 8
GDN_CHUNK = 128
GDN_GROUP = 4
TOKEN_TILE = 512
FFN_COLS = 256
Q_TILE = 256
ATTN_HEADS_PER_STEP = 4
VMEM_LIMIT = 56 * 1024 * 1024

COL_QKV = 0
COL_GATE = 3 * GROUP_W
COL_QD = COL_GATE + GROUP_W
COL_KD = COL_QD + GROUP_W
COL_VD = COL_KD + GROUP_W
COL_AB = COL_VD + GROUP_W
W_ALL_COLS = COL_AB + HEAD_W


def _dot(a, b):
    return jnp.dot(a, b, preferred_element_type=F32)


def _dot_nt(a, b):
    return lax.dot_general(a, b, (((1,), (1,)), ((), ())), preferred_element_type=F32)


def _rms(x, w):
    return x * lax.rsqrt(jnp.mean(x * x, axis=-1, keepdims=True) + EPS) * w


def _sigmoid(x):
    return 1.0 / (1.0 + jnp.exp(-x))


def _silu(x):
    return x * _sigmoid(x)


def _resident(arr, layer=None):
    if layer is None:
        shape, index = arr.shape, (0,) * arr.ndim
    else:
        shape, index = (None,) + arr.shape[1:], (layer,) + (0,) * (arr.ndim - 1)
    return pl.BlockSpec(shape, lambda *_: index, pipeline_mode=pl.Buffered(1))


def _ffn_body(*refs, pre_proj, post_norm):
    it = iter(refs)
    x_ref = next(it)
    if pre_proj:
        oa_ref, od_ref, wo_ref = next(it), next(it), next(it)
    nw_ref, wg_ref, wu_ref, wd_ref = next(it), next(it), next(it), next(it)
    if post_norm:
        fw_ref = next(it)
    o_ref, a_scr = next(it), next(it)

    x = x_ref[...]
    if pre_proj:
        x = x + _dot(oa_ref[...], wo_ref[0:GROUP_W, :]) + _dot(od_ref[...], wo_ref[GROUP_W:2 * GROUP_W, :])
    h = _rms(x, nw_ref[...]).astype(BF16)
    for c in range(wg_ref.shape[1] // FFN_COLS):
        cols = slice(c * FFN_COLS, (c + 1) * FFN_COLS)
        g = _dot(h, wg_ref[:, cols])
        u = _dot(h, wu_ref[:, cols])
        a_scr[:, cols] = (_silu(g) * u).astype(BF16)
    out = x + 0.5 * _dot(a_scr[...], wd_ref[...])
    if post_norm:
        out = _rms(out, fw_ref[...])
    o_ref[...] = out


def _ffn(x, nw, wg, wu, wd, *, layer, proj=None, final_w=None, tm):
    t, d = x.shape
    d_ff = wg.shape[2]
    row = lambda w: pl.BlockSpec((tm, w), lambda i: (i, 0))
    args, specs = [x], [row(d)]
    if proj is not None:
        oa, od, wo = proj
        args += [oa, od, wo]
        specs += [row(GROUP_W), row(GROUP_W), _resident(wo, layer)]
    args += [nw, wg, wu, wd]
    specs += [_resident(a, layer) for a in (nw, wg, wu, wd)]
    if final_w is not None:
        args.append(final_w)
        specs.append(_resident(final_w))
    return pl.pallas_call(
        functools.partial(_ffn_body, pre_proj=proj is not None, post_norm=final_w is not None),
        grid=(t // tm,),
        in_specs=specs,
        out_specs=row(d),
        out_shape=jax.ShapeDtypeStruct((t, d), F32),
        scratch_shapes=[pltpu.VMEM((tm, d_ff), BF16)],
        compiler_params=pltpu.CompilerParams(dimension_semantics=("parallel",), vmem_limit_bytes=VMEM_LIMIT),
        name="ffn_proj" if proj is not None else "ffn",
    )(*args)


def _split3(x):
    hi = x.astype(BF16)
    r = x - hi.astype(F32)
    mid = r.astype(BF16)
    lo = (r - mid.astype(F32)).astype(BF16)
    return hi, mid, lo


def _in_proj_body(x_ref, nw_ref, w_ref, cw_ref, alog_ref, dtb_ref, cos_ref, sin_ref,
                  qa_ref, ka_ref, va_ref, gate_ref, gcb_ref, gct_ref, qd_ref, kd_ref, vd_ref,
                  pbuf, *, blocks_per_seq):
    tm = x_ref.shape[0]

    @pl.when(pl.program_id(0) % blocks_per_seq == 0)
    def _():
        pbuf[0:HALO, :] = jnp.zeros((HALO, pbuf.shape[1]), F32)

    h = _rms(x_ref[...], nw_ref[...]).astype(BF16)

    def project(col, width=GROUP_W):
        return _dot(h, w_ref[:, col:col + width])

    def conv_section(sec, out_ref):
        cols = slice(sec * GROUP_W, (sec + 1) * GROUP_W)
        p = project(COL_QKV + sec * GROUP_W)
        pbuf[HALO:HALO + tm, cols] = p
        y = p * cw_ref[CONV_K - 1:CONV_K, cols]
        for j in range(CONV_K - 1):
            shift = CONV_K - 1 - j
            y = y + pbuf[HALO - shift:HALO - shift + tm, cols] * cw_ref[j:j + 1, cols]
        pbuf[0:HALO, cols] = pbuf[tm:tm + HALO, cols]
        y = _silu(y)
        if sec < 2:
            scale = HEAD_W ** -0.5 if sec == 0 else 1.0
            for hd in range(N_HEADS):
                hc = slice(hd * HEAD_W, (hd + 1) * HEAD_W)
                yh = y[:, hc]
                yh = yh * (lax.rsqrt(jnp.sum(yh * yh, axis=-1, keepdims=True) + EPS) * scale)
                out_ref[:, hc] = yh.astype(BF16)
        else:
            out_ref[...] = y.astype(BF16)

    def rope_section(col, out_ref, scale):
        cos = cos_ref[...]
        sin = sin_ref[...]
        low_half = (lax.broadcasted_iota(jnp.int32, (tm, HEAD_W), 1) % DIFF_D) < (DIFF_D // 2)
        p = project(col)
        for hd in range(N_HEADS):
            hc = slice(hd * HEAD_W, (hd + 1) * HEAD_W)
            ph = p[:, hc]
            partner = jnp.where(low_half,
                                pltpu.roll(ph, HEAD_W - DIFF_D // 2, 1),
                                pltpu.roll(ph, DIFF_D // 2, 1))
            out_ref[:, hc] = ((ph * cos + partner * sin) * scale).astype(BF16)

    conv_section(0, qa_ref)
    conv_section(1, ka_ref)
    conv_section(2, va_ref)
    gate_ref[...] = project(COL_GATE).astype(BF16)

    ab = project(COL_AB, HEAD_W)
    z = ab + dtb_ref[...]
    softplus = jnp.maximum(z, 0.0) + jnp.log1p(jnp.exp(-jnp.abs(z)))
    lane = lax.broadcasted_iota(jnp.int32, ab.shape, 1)
    g = jnp.where(lane < N_HEADS, -jnp.exp(alog_ref[...]) * softplus, 0.0)
    r = lax.broadcasted_iota(jnp.int32, (GDN_CHUNK, GDN_CHUNK), 0)
    cidx = lax.broadcasted_iota(jnp.int32, (GDN_CHUNK, GDN_CHUNK), 1)
    tri = jnp.where(cidx <= r, 1.0, 0.0).astype(BF16)
    beta = _sigmoid(ab)
    decay_lane = lax.broadcasted_iota(jnp.int32, (GDN_CHUNK, HEAD_W), 1) < N_HEADS
    for cc in range(tm // GDN_CHUNK):
        blk = slice(cc * GDN_CHUNK, (cc + 1) * GDN_CHUNK)
        gc = sum(_dot(tri, part) for part in _split3(g[blk, :]))
        gcb_ref[blk, :] = jnp.where(decay_lane, gc, beta[blk, :])
        gct_ref[:, blk] = gc.T[0:HALO, :]

    rope_section(COL_QD, qd_ref, DIFF_D ** -0.5 * LOG2_E)
    rope_section(COL_KD, kd_ref, 1.0)
    vd_ref[...] = project(COL_VD).astype(BF16)


def _in_proj(x, nw, w_all, cw, alog, dtb, cos, sin, *, layer, seq, tm):
    t, d = x.shape
    blocks_per_seq = seq // tm
    row = lambda w: pl.BlockSpec((tm, w), lambda i: (i, 0))
    tab = pl.BlockSpec((tm, HEAD_W), lambda i: (i % blocks_per_seq, 0))
    res = [nw, w_all, cw, alog, dtb]
    bf = lambda: jax.ShapeDtypeStruct((t, GROUP_W), BF16)
    return pl.pallas_call(
        functools.partial(_in_proj_body, blocks_per_seq=blocks_per_seq),
        grid=(t // tm,),
        in_specs=[row(d)] + [_resident(a, layer) for a in res] + [tab, tab],
        out_specs=[row(GROUP_W)] * 4 + [row(HEAD_W), pl.BlockSpec((HALO, tm), lambda i: (0, i))] + [row(GROUP_W)] * 3,
        out_shape=[bf(), bf(), bf(), bf(), jax.ShapeDtypeStruct((t, HEAD_W), F32),
                   jax.ShapeDtypeStruct((HALO, t), F32), bf(), bf(), bf()],
        scratch_shapes=[pltpu.VMEM((tm + HALO, 3 * GROUP_W), F32)],
        compiler_params=pltpu.CompilerParams(dimension_semantics=("arbitrary",), vmem_limit_bytes=VMEM_LIMIT),
        name="in_proj",
    )(x, *res, cos, sin)


def _block_diag(x):
    half = x.shape[1] // 2
    z = jnp.zeros((x.shape[0], half), x.dtype)
    return jnp.concatenate([jnp.concatenate([x[:, :half], z], axis=1),
                            jnp.concatenate([z, x[:, half:]], axis=1)], axis=0)


def _gdn_body(q_ref, k_ref, v_ref, gate_ref, gcb_ref, gct_ref, nw_ref, o_ref,
              u_scr, w_scr, qd_scr, qk_scr, kdt_scr, erow_scr, state, *, group):
    c = GDN_CHUNK
    pw = 2 * HEAD_W
    n_pairs = N_HEADS // 2
    n_chunks = q_ref.shape[0] // c
    levels = int(math.log2(c))
    row = lax.broadcasted_iota(jnp.int32, (c, pw), 0)
    col = lax.broadcasted_iota(jnp.int32, (c, pw), 1) & (HEAD_W - 1)
    incl = row >= col
    strict = row > col
    eye = jnp.where(row == col, 1.0, 0.0)
    xor = row ^ col
    pair_level = jnp.full((c, pw), -1, jnp.int32)
    for level in range(levels):
        pair_level = jnp.where((xor >> level) == 1, level, pair_level)

    def bcast_pair(arr, lane0, lane1):
        return jnp.concatenate([jnp.broadcast_to(arr[:, lane0:lane0 + 1], (c, HEAD_W)),
                                jnp.broadcast_to(arr[:, lane1:lane1 + 1], (c, HEAD_W))], axis=1)

    def odd_rows(a, level):
        b = 1 << level
        return jnp.concatenate([a[r * b:(r + 1) * b] for r in range(1, c // b, 2)], axis=0)

    def spread_odd(a_odd, other, level):
        b = 1 << level
        return jnp.concatenate([a_odd[(r // 2) * b:(r // 2 + 1) * b] if r % 2 else other[r * b:(r + 1) * b]
                                for r in range(c // b)], axis=0)

    zeros_pair = jnp.zeros((c, pw), F32)
    vreg_level = 3

    def setup(ci):
        rows = slice(ci * c, (ci + 1) * c)
        gcb = gcb_ref[rows, :]
        g_last = gcb[c - 1:c, :]
        e_gc = jnp.exp(gcb)
        e_rest = jnp.exp(g_last - gcb)
        e_last = jnp.exp(g_last)
        erow_scr[ci * HALO:(ci + 1) * HALO, :] = jnp.concatenate(
            [jnp.broadcast_to(e_last[:, hd:hd + 1], (HALO, HEAD_W)) for hd in range(N_HEADS)], axis=1)
        chains = []
        for p in range(n_pairs):
            h0, h1 = 2 * p, 2 * p + 1
            cols = slice(p * pw, (p + 1) * pw)
            q = q_ref[rows, cols]
            k = k_ref[rows, cols]
            kf = k.astype(F32)
            beta = bcast_pair(gcb, N_HEADS + h0, N_HEADS + h1)
            e_gc_p = bcast_pair(e_gc, h0, h1)
            gc_col = bcast_pair(gcb, h0, h1)
            gc_row = jnp.concatenate([jnp.broadcast_to(gct_ref[h0:h0 + 1, rows], (c, HEAD_W)),
                                      jnp.broadcast_to(gct_ref[h1:h1 + 1, rows], (c, HEAD_W))], axis=1)
            decay = jnp.where(incl, jnp.exp(gc_col - gc_row), 0.0)
            kb = kf * beta
            prod = _dot_nt(jnp.concatenate([kb.astype(BF16), q], axis=0), _block_diag(k))
            m = jnp.where(strict, prod[:c] * decay, 0.0)
            qk_scr[rows, cols] = (prod[c:] * decay).astype(BF16)
            qd_scr[rows, cols] = (q.astype(F32) * e_gc_p).astype(BF16)
            kd = kf * bcast_pair(e_rest, h0, h1)
            kdt_scr[rows, cols] = jnp.concatenate([kd[:, :HEAD_W].T, kd[:, HEAD_W:].T], axis=1).astype(BF16)
            neg_m = -m
            chains.append(dict(
                rows=rows, cols=cols, neg_m=neg_m.astype(BF16),
                neg_m_odd={lv: odd_rows(neg_m, lv).astype(BF16) for lv in range(vreg_level, levels)},
                vb=(v_ref[rows, cols].astype(F32) * beta).astype(BF16),
                kbe=(kb * e_gc_p).astype(BF16),
                t=eye - jnp.where(pair_level == 0, m, 0.0)))
        return chains

    def level_products(chains, level):
        for ch in chains:
            ch["t_bf"] = ch["t"].astype(BF16)
            if level < vreg_level:
                ch["x"] = _dot(ch["neg_m"], _block_diag(ch["t_bf"])).astype(BF16)
            else:
                x_odd = _dot(ch["neg_m_odd"][level], _block_diag(ch["t_bf"]))
                ch["x"] = spread_odd(x_odd, zeros_pair, level).astype(BF16)

    def level_update(chains, level):
        for ch in chains:
            if level < vreg_level:
                y = _dot(ch["t_bf"], _block_diag(ch["x"]))
                ch["t"] = jnp.where(pair_level == level, y, ch["t"])
            else:
                t_odd = odd_rows(ch["t"], level)
                y_odd = _dot(t_odd.astype(BF16), _block_diag(ch["x"]))
                t_odd = jnp.where(odd_rows(pair_level, level) == level, y_odd, t_odd)
                ch["t"] = spread_odd(t_odd, ch["t"], level)

    def solve(chains):
        for ch in chains:
            t_bf = ch["t"].astype(BF16)
            u_scr[ch["rows"], ch["cols"]] = _dot(t_bf, _block_diag(ch["vb"]))
            w_scr[ch["rows"], ch["cols"]] = _dot(t_bf, _block_diag(ch["kbe"])).astype(BF16)

    nw = jnp.concatenate([nw_ref[...], nw_ref[...]], axis=1)
    state[...] = jnp.zeros(state.shape, F32)

    def recur(i):
        rows = slice(i * c, (i + 1) * c)
        e_row = erow_scr[i * HALO:i * HALO + 1, :]
        for p in range(n_pairs):
            cols = slice(p * pw, (p + 1) * pw)
            s = state[p]
            r1 = _dot(jnp.concatenate([w_scr[rows, cols], qd_scr[rows, cols]], axis=0),
                      _block_diag(s.astype(BF16)))
            v_new = (u_scr[rows, cols] - r1[:c]).astype(BF16)
            r2 = _dot(jnp.concatenate([kdt_scr[rows, cols], qk_scr[rows, cols]], axis=0),
                      _block_diag(v_new))
            state[p] = s * e_row[:, cols] + r2[:c]
            o = r1[c:] + r2[c:]
            ms = jnp.concatenate(
                [jnp.broadcast_to(jnp.mean(o[:, hh * HEAD_W:(hh + 1) * HEAD_W] ** 2, axis=-1, keepdims=True),
                                  (c, HEAD_W)) for hh in range(2)], axis=1)
            o = o * lax.rsqrt(ms + EPS) * nw * _silu(gate_ref[rows, cols].astype(F32))
            o_ref[rows, cols] = o.astype(BF16)

    n_groups = n_chunks // group
    prepared = {}
    for it in range(n_groups + 2):
        mxu_steps = []
        if 0 <= it - 1 < n_groups:
            chains = prepared.pop(it - 1)
            for level in range(1, levels):
                mxu_steps.append(functools.partial(level_products, chains, level))
                mxu_steps.append(functools.partial(level_update, chains, level))
            mxu_steps.append(functools.partial(solve, chains))
        fill_steps = []
        if it < n_groups:
            prepared[it] = []
            for ci in range(it * group, (it + 1) * group):
                fill_steps.append(lambda ci=ci, it=it: prepared[it].extend(setup(ci)))
        if 0 <= it - 2 < n_groups:
            rec = [functools.partial(recur, i) for i in range((it - 2) * group, (it - 1) * group)]
            fill_steps = [s for pair in zip(fill_steps, rec) for s in pair] if fill_steps else rec
        stride = max(1, len(mxu_steps) // max(1, len(fill_steps)))
        for n, step in enumerate(mxu_steps):
            step()
            if n % stride == stride - 1 and fill_steps:
                fill_steps.pop(0)()
        for step in fill_steps:
            step()


def _gdn(q, k, v, gate, gcb, gct, nw, *, layer, seq):
    t = q.shape[0]
    n_chunks = seq // GDN_CHUNK
    seq_blk = lambda w: pl.BlockSpec((seq, w), lambda b: (b, 0))
    return pl.pallas_call(
        functools.partial(_gdn_body, group=math.gcd(GDN_GROUP, n_chunks)),
        grid=(t // seq,),
        in_specs=[seq_blk(GROUP_W)] * 4 + [seq_blk(HEAD_W), pl.BlockSpec((HALO, seq), lambda b: (0, b)),
                                           _resident(nw, layer)],
        out_specs=seq_blk(GROUP_W),
        out_shape=jax.ShapeDtypeStruct((t, GROUP_W), BF16),
        scratch_shapes=[pltpu.VMEM((seq, GROUP_W), F32)] + [pltpu.VMEM((seq, GROUP_W), BF16)] * 4
                       + [pltpu.VMEM((n_chunks * HALO, GROUP_W), F32),
                          pltpu.VMEM((N_HEADS // 2, GDN_CHUNK, 2 * HEAD_W), F32)],
        compiler_params=pltpu.CompilerParams(dimension_semantics=("parallel",), vmem_limit_bytes=VMEM_LIMIT),
        name="gdn",
    )(q, k, v, gate, gcb, gct, nw)


def _block_order(n):
    long_first = list(range(n - 1, -1, -1))
    head, rest = long_first[:3], long_first[3:]
    order = list(head)
    while rest:
        order.append(rest.pop())
        if rest:
            order.append(rest.pop(0))
    return order


def _diff_body(q_ref, k_ref, v_ref, lq1_ref, lk1_ref, lq2_ref, lk2_ref, nw_ref, o_ref,
               v1_scr, *, lam_init, tq, heads):
    seq = q_ref.shape[0]
    rows2 = 2 * tq
    lane = lax.broadcasted_iota(jnp.int32, (tq, HEAD_W), 1)
    unit = jnp.where(lax.broadcasted_iota(jnp.int32, (seq, HEAD_W), 1) == 0, 1.0, 0.0).astype(BF16)
    for hd in range(heads):
        v1_scr[:, 2 * hd * HEAD_W:(2 * hd + 1) * HEAD_W] = v_ref[:, hd * HEAD_W:(hd + 1) * HEAD_W]
        v1_scr[:, (2 * hd + 1) * HEAD_W:(2 * hd + 2) * HEAD_W] = unit
    r = lax.broadcasted_iota(jnp.int32, (rows2, tq), 0) & (tq - 1)
    keep = lax.broadcasted_iota(jnp.int32, (rows2, tq), 1) <= r
    lam = (jnp.exp(jnp.sum(lq1_ref[...] * lk1_ref[...], axis=-1, keepdims=True))
           - jnp.exp(jnp.sum(lq2_ref[...] * lk2_ref[...], axis=-1, keepdims=True)) + lam_init)
    nw = nw_ref[...]

    for i in _block_order(seq // tq):
        for hd in range(heads):
            hc = slice(hd * HEAD_W, (hd + 1) * HEAD_W)
            q_rows = slice(i * tq, (i + 1) * tq)
            n_keys = (i + 1) * tq
            q = q_ref[q_rows, hc]
            zero = jnp.zeros_like(q)
            q2 = jnp.concatenate([jnp.where(lane < DIFF_D, q, zero), jnp.where(lane >= DIFF_D, q, zero)], axis=0)
            s = _dot_nt(q2, k_ref[0:n_keys, hc])
            s_diag = jnp.where(keep, s[:, i * tq:], -jnp.inf)
            s = s_diag if i == 0 else jnp.concatenate([s[:, :i * tq], s_diag], axis=1)
            m = jnp.max(s, axis=-1, keepdims=True)
            p = jnp.exp2((s - m).astype(BF16))
            acc = _dot(p, v1_scr[0:n_keys, 2 * hd * HEAD_W:(2 * hd + 2) * HEAD_W])
            o = acc[:, :HEAD_W] / acc[:, HEAD_W:HEAD_W + 1]
            o = o[:tq] - lam * o[tq:]
            o_ref[q_rows, hc] = (_rms(o, nw) * (1.0 - lam_init)).astype(BF16)


def _diff_attn(q, k, v, lq1, lk1, lq2, lk2, nw, *, layer, seq, lam_init, tq):
    t = q.shape[0]
    heads = ATTN_HEADS_PER_STEP
    blk = pl.BlockSpec((seq, heads * HEAD_W), lambda b, h: (b, h))
    small = [lq1, lk1, lq2, lk2, nw]
    return pl.pallas_call(
        functools.partial(_diff_body, lam_init=lam_init, tq=tq, heads=heads),
        grid=(t // seq, N_HEADS // heads),
        in_specs=[blk, blk, blk] + [_resident(a, layer) for a in small],
        out_specs=blk,
        out_shape=jax.ShapeDtypeStruct((t, GROUP_W), BF16),
        scratch_shapes=[pltpu.VMEM((seq, 2 * heads * HEAD_W), BF16)],
        compiler_params=pltpu.CompilerParams(
            dimension_semantics=("parallel", "parallel"), vmem_limit_bytes=VMEM_LIMIT),
        name="diff_attn",
    )(q, k, v, *small)


def _rope_tables(seq):
    inv = 1.0 / (ROPE_THETA ** (np.arange(0, DIFF_D, 2, dtype=np.float32) / DIFF_D))
    ang = np.arange(seq, dtype=np.float32)[:, None] * inv[None, :].astype(np.float32)
    cos, sin = np.cos(ang).astype(np.float32), np.sin(ang).astype(np.float32)
    reps = HEAD_W // DIFF_D
    return (jnp.asarray(np.tile(np.concatenate([cos, cos], axis=-1), (1, reps))),
            jnp.asarray(np.tile(np.concatenate([-sin, sin], axis=-1), (1, reps))))


def _lambda_init(layer):
    return 0.8 - 0.6 * math.exp(-0.3 * layer)


def kernel(x, ffn1_norm, ffn1_w_gate, ffn1_w_up, ffn1_w_down, mix_norm, w_in, conv_w, a_log, dt_bias, gdn_norm, lambda_q1, lambda_k1, lambda_q2, lambda_k2, diff_norm, w_out, ffn2_norm, ffn2_w_gate, ffn2_w_up, ffn2_w_down, final_norm):
    b, seq, d = x.shape
    depth = w_in.shape[0]
    t = b * seq
    tm = min(TOKEN_TILE, seq)
    tq = min(Q_TILE, seq)
    assert seq % GDN_CHUNK == 0 and seq % tm == 0 and seq % tq == 0
    n_qkv_gate = 4 * GROUP_W
    assert w_in.shape[2] == n_qkv_gate + 2 * N_HEADS + 3 * GROUP_W
    cos, sin = _rope_tables(seq)
    bf = lambda a: a.astype(BF16)
    rows = lambda a: a.reshape(depth, 1, -1).astype(F32)
    pad_lanes = lambda a: jnp.pad(rows(a), ((0, 0), (0, 0), (0, HEAD_W - a.shape[1])))

    w_in_bf = bf(w_in)
    w_all = jnp.concatenate(
        [w_in_bf[:, :, :n_qkv_gate], w_in_bf[:, :, n_qkv_gate + 2 * N_HEADS:],
         w_in_bf[:, :, n_qkv_gate:n_qkv_gate + 2 * N_HEADS],
         jnp.zeros((depth, d, HEAD_W - 2 * N_HEADS), BF16)], axis=2)
    ffn1 = (rows(ffn1_norm), bf(ffn1_w_gate), bf(ffn1_w_up), bf(ffn1_w_down))
    ffn2 = (rows(ffn2_norm), bf(ffn2_w_gate), bf(ffn2_w_up), bf(ffn2_w_down))
    wo = bf(w_out)
    mix_nw, gdn_nw, diff_nw = rows(mix_norm), rows(gdn_norm), rows(diff_norm)
    lams = [rows(a) for a in (lambda_q1, lambda_k1, lambda_q2, lambda_k2)]
    alog, dtb = pad_lanes(a_log), pad_lanes(dt_bias)
    cw = conv_w.astype(F32)

    xf = x.reshape(t, d).astype(F32)
    xf = _ffn(xf, *ffn1, layer=0, tm=tm)
    for l in range(depth):
        qa, ka, va, gate, gcb, gct, qd, kd, vd = _in_proj(
            xf, mix_nw, w_all, cw, alog, dtb, cos, sin, layer=l, seq=seq, tm=tm)
        o_a = _gdn(qa, ka, va, gate, gcb, gct, gdn_nw, layer=l, seq=seq)
        o_d = _diff_attn(qd, kd, vd, *lams, diff_nw, layer=l, seq=seq, lam_init=_lambda_init(l), tq=tq)
        last = l == depth - 1
        xf = _ffn(xf, *ffn2, layer=l, proj=(o_a, o_d, wo),
                  final_w=final_norm.reshape(1, d).astype(F32) if last else None, tm=tm)
        if not last:
            xf = _ffn(xf, *ffn1, layer=l + 1, tm=tm)
    return xf.reshape(b, seq, d).astype(x.dtype)
```
